```python
import math
import jax, jax.numpy as jnp
from jax import lax
import numpy as np

D_MODEL = 1024
BATCH = 8
SEQ = 2048
DEPTH = 4

CTX_LEN = 256
GRID_W = 64
DIFF_HEADS = 4
DIFF_HEAD_DIM = 64
DIFF_WIDTH = DIFF_HEADS * 2 * DIFF_HEAD_DIM
POOL_WINDOWS = (2, 4, 8, 16)
POOL_GROUPS = 4
POOL_WIDTH = 512
POOL_GROUP_DIM = POOL_WIDTH // POOL_GROUPS
NAT_HEADS = 8
NAT_HEAD_DIM = 64
NAT_WIDTH = NAT_HEADS * NAT_HEAD_DIM
WIN_ROWS = 8
WIN_COLS = 16
N_BRANCHES = 3
IN_COLS = 3 * DIFF_WIDTH + POOL_WIDTH + 3 * NAT_WIDTH + N_BRANCHES * D_MODEL
D_FF = 4 * D_MODEL
Q_BLOCK = 128
ROPE_BASE = 10000.0
NORM_EPS = 1e-6

kernel_name = "hybrid_gated_diffattn_pool_natten_block"


def rms_norm(x, g):
    xf = x.astype(jnp.float32)
    y = xf * lax.rsqrt(jnp.mean(xf * xf, axis=-1, keepdims=True) + NORM_EPS)
    return (y * g.astype(jnp.float32)).astype(x.dtype)


def modulate(h, shift, scale):
    return h * (1 + scale) + shift


def adaln(cond, w, b):
    m = jax.nn.silu(cond) @ w + b
    m = m.reshape(m.shape[:-1] + (1, 6, D_MODEL))
    return tuple(m[..., i, :] for i in range(6))


def split_in(z):
    sizes = (DIFF_WIDTH,) * 3 + (POOL_WIDTH,) + (NAT_WIDTH,) * 3
    idx = [int(v) for v in np.cumsum(sizes)]
    return jnp.split(z, idx, axis=-1)


def axial_rope(x):
    L, dh = x.shape[1], x.shape[-1]
    half = dh // 2
    nf = half // 2
    pos = jnp.arange(L)
    row = (pos // GRID_W).astype(jnp.float32)
    col = (pos % GRID_W).astype(jnp.float32)
    inv = ROPE_BASE ** (-jnp.arange(nf, dtype=jnp.float32) / nf)

    def rot(u, p):
        ang = p[:, None] * inv
        cos = jnp.cos(ang)[None, :, None, None, :].astype(x.dtype)
        sin = jnp.sin(ang)[None, :, None, None, :].astype(x.dtype)
        u1, u2 = u[..., :nf], u[..., nf:]
        return jnp.concatenate([u1 * cos - u2 * sin, u1 * sin + u2 * cos], axis=-1)

    return jnp.concatenate([rot(x[..., :half], row), rot(x[..., half:], col)], axis=-1)


def diff_attention(q, k, v, lam):
    B, Lq, H, _, dh = q.shape
    nb = Lq // Q_BLOCK
    qb = q.reshape(B, nb, Q_BLOCK, H, 2, dh).swapaxes(0, 1)
    sc = dh ** -0.5

    def block(qq):
        s = jnp.einsum('bqhmd,bkhmd->bhmqk', qq, k).astype(jnp.float32) * sc
        p = jax.nn.softmax(s, axis=-1)
        a = (p[:, :, 0] - lam * p[:, :, 1]).astype(v.dtype)
        return jnp.einsum('bhqk,bkhe->bqhe', a, v)

    o = lax.map(block, qb)
    return o.swapaxes(0, 1).reshape(B, Lq, H, v.shape[-1])


def diff_head_out(o, g_subln, lam_init):
    B, L = o.shape[0], o.shape[1]
    return (rms_norm(o, g_subln) * (1.0 - lam_init)).reshape(B, L, DIFF_WIDTH)


def dense_attention(q, k, v):
    s = jnp.einsum('bqhd,bkhd->bhqk', q, k).astype(jnp.float32) * (q.shape[-1] ** -0.5)
    p = jax.nn.softmax(s, axis=-1).astype(v.dtype)
    return jnp.einsum('bhqk,bkhd->bqhd', p, v)


def neighbourhood_attention(q, k, v, kc, vc, rpb):
    B, L, H, d = q.shape
    rows = L // GRID_W
    kr = min(WIN_ROWS, rows)
    sc = d ** -0.5
    qg = q.reshape(B, rows, GRID_W, H, d)
    kg = k.reshape(B, rows, GRID_W, H, d)
    vg = v.reshape(B, rows, GRID_W, H, d)
    r = jnp.arange(rows)
    row_idx = jnp.clip(r - kr // 2, 0, rows - kr)[:, None] + jnp.arange(kr)
    cq = jnp.arange(GRID_W)
    col_idx = jnp.clip(cq - WIN_COLS // 2, 0, GRID_W - WIN_COLS)[:, None] + jnp.arange(WIN_COLS)
    d_row = row_idx - r[:, None] + (WIN_ROWS - 1)
    d_col = col_idx - cq[:, None] + (WIN_COLS - 1)
    rpb_c = rpb[:, :, d_col]
    n_loc = kr * WIN_COLS

    def one_row(args):
        q_r, ridx, dr = args
        k_win = kg[:, ridx][:, :, col_idx]
        v_win = vg[:, ridx][:, :, col_idx]
        bias = rpb_c[:, dr].transpose(0, 2, 1, 3).astype(jnp.float32)
        s_loc = jnp.einsum('bwhd,brwkhd->bhwrk', q_r, k_win).astype(jnp.float32) * sc + bias
        s_ctx = jnp.einsum('bwhd,bchd->bhwc', q_r, kc).astype(jnp.float32) * sc
        s = jnp.concatenate([s_loc.reshape(B, H, GRID_W, n_loc), s_ctx], axis=-1)
        p = jax.nn.softmax(s, axis=-1).astype(v.dtype)
        p_loc = p[..., :n_loc].reshape(B, H, GRID_W, kr, WIN_COLS)
        p_ctx = p[..., n_loc:]
        return (jnp.einsum('bhwrk,brwkhd->bwhd', p_loc, v_win)
                + jnp.einsum('bhwc,bchd->bwhd', p_ctx, vc))

    o = lax.map(one_row, (qg.swapaxes(0, 1), row_idx, d_row))
    return o.swapaxes(0, 1).reshape(B, L, H, d)


def multiscale_pool(u, group_w, chan_scale):
    B, L, _ = u.shape
    uf = u.astype(jnp.float32)
    csum = jnp.concatenate([jnp.zeros((B, 1, POOL_WIDTH), jnp.float32), jnp.cumsum(uf, axis=1)], axis=1)
    t = jnp.arange(L)
    parts = []
    for gi, w in enumerate(POOL_WINDOWS):
        sl = slice(gi * POOL_GROUP_DIM, (gi + 1) * POOL_GROUP_DIM)
        lo = jnp.clip(t - w // 2, 0, L)
        hi = jnp.clip(t + w // 2, 0, L)
        cnt = (hi - lo).astype(jnp.float32)[None, :, None]
        cg = csum[:, :, sl]
        parts.append((cg[:, hi] - cg[:, lo]) / cnt - uf[:, :, sl])
    p = jnp.concatenate(parts, axis=-1).astype(u.dtype).reshape(B, L, POOL_GROUPS, POOL_GROUP_DIM)
    y = jnp.einsum('blgc,gce->blge', p, group_w).reshape(B, L, POOL_WIDTH)
    return y * chan_scale


def merge_branches(o_a, o_b, o_c, gate_logits, w_branch, w_out):
    g = jax.nn.sigmoid(gate_logits.astype(jnp.float32)).astype(o_a.dtype)
    g = g.reshape(gate_logits.shape[:-1] + (N_BRANCHES, D_MODEL))
    merged = (g[..., 0, :] * (o_a @ w_branch[0])
              + g[..., 1, :] * (o_b @ w_branch[1])
              + g[..., 2, :] * (o_c @ w_branch[2]))
    return merged @ w_out


def sq_relu_mlp(h, w1, w2):
    return jnp.square(jax.nn.relu(h @ w1)) @ w2


def setup_inputs(seed: int = 0) -> dict:
    key = jax.random.key(seed)
    ks = jax.random.split(key, 24)

    def nrm(k, shape, std):
        return jax.random.normal(k, shape, jnp.float32) * std

    dh = DIFF_HEAD_DIM
    return {
        "x": nrm(ks[0], (BATCH, SEQ, D_MODEL), 1.0),
        "c": nrm(ks[1], (BATCH, D_MODEL), 1.0),
        "ctx": nrm(ks[2], (BATCH, CTX_LEN, D_MODEL), 1.0),
        "c_ctx": nrm(ks[3], (D_MODEL,), 1.0),
        "w_ada": nrm(ks[4], (DEPTH, D_MODEL, 6 * D_MODEL), 0.5 * D_MODEL ** -0.5),
        "b_ada": nrm(ks[5], (DEPTH, 6 * D_MODEL), 0.01),
        "g_mix": 1.0 + nrm(ks[6], (DEPTH, D_MODEL), 0.05),
        "g_mlp": 1.0 + nrm(ks[7], (DEPTH, D_MODEL), 0.05),
        "w_in": nrm(ks[8], (DEPTH, D_MODEL, IN_COLS), D_MODEL ** -0.5),
        "lambda_q1": nrm(ks[9], (DEPTH, dh), 0.1),
        "lambda_k1": nrm(ks[10], (DEPTH, dh), 0.1),
        "lambda_q2": nrm(ks[11], (DEPTH, dh), 0.1),
        "lambda_k2": nrm(ks[12], (DEPTH, dh), 0.1),
        "g_subln": 1.0 + nrm(ks[13], (DEPTH, 2 * dh), 0.05),
        "pool_w": nrm(ks[14], (DEPTH, POOL_GROUPS, POOL_GROUP_DIM, POOL_GROUP_DIM), POOL_GROUP_DIM ** -0.5),
        "pool_scale": 1.0 + nrm(ks[15], (DEPTH, POOL_WIDTH), 0.1),
        "nat_rpb": nrm(ks[16], (DEPTH, NAT_HEADS, 2 * WIN_ROWS - 1, 2 * WIN_COLS - 1), 0.1),
        "w_branch": nrm(ks[17], (DEPTH, N_BRANCHES, DIFF_WIDTH, D_MODEL), DIFF_WIDTH ** -0.5),
        "w_out": nrm(ks[18], (DEPTH, D_MODEL, D_MODEL), D_MODEL ** -0.5),
        "w_mlp1": nrm(ks[19], (DEPTH, D_MODEL, D_FF), D_MODEL ** -0.5),
        "w_mlp2": nrm(ks[20], (DEPTH, D_FF, D_MODEL), D_FF ** -0.5),
        "g_final": 1.0 + nrm(ks[21], (D_MODEL,), 0.05),
    }


def reference(x, c, ctx, c_ctx, w_ada, b_ada, g_mix, g_mlp, w_in, lambda_q1, lambda_k1, lambda_q2, lambda_k2,
              g_subln, pool_w, pool_scale, nat_rpb, w_branch, w_out, w_mlp1, w_mlp2, g_final):
    B, L, _ = x.shape
    C = ctx.shape[1]
    H, dh = DIFF_HEADS, DIFF_HEAD_DIM
    xc = ctx
    for l in range(DEPTH):
        last = l == DEPTH - 1
        lam_init = 0.8 - 0.6 * math.exp(-0.3 * l)
        f32 = jnp.float32
        lam = (jnp.exp(jnp.sum(lambda_q1[l].astype(f32) * lambda_k1[l].astype(f32)))
               - jnp.exp(jnp.sum(lambda_q2[l].astype(f32) * lambda_k2[l].astype(f32))) + lam_init)
        sh1, sc1, ga1, sh2, sc2, ga2 = adaln(c, w_ada[l], b_ada[l])
        csh1, csc1, cga1, csh2, csc2, cga2 = adaln(c_ctx, w_ada[l], b_ada[l])

        h = modulate(rms_norm(x, g_mix[l]), sh1, sc1)
        hc = modulate(rms_norm(xc, g_mix[l]), csh1, csc1)
        qa, ka, va, pu, qn, kn, vn, gl = split_in(h @ w_in[l])
        qac, kac, vac, puc, qnc, knc, vnc, glc = split_in(hc @ w_in[l])

        qa = axial_rope(qa.reshape(B, L, H, 2, dh))
        ka = axial_rope(ka.reshape(B, L, H, 2, dh))
        va = va.reshape(B, L, H, 2 * dh)
        kac = kac.reshape(B, C, H, 2, dh)
        vac = vac.reshape(B, C, H, 2 * dh)
        o_a = diff_attention(qa, jnp.concatenate([kac, ka], axis=1), jnp.concatenate([vac, va], axis=1), lam)
        o_a = diff_head_out(o_a, g_subln[l], lam_init)

        o_b = multiscale_pool(pu, pool_w[l], pool_scale[l])

        knc = knc.reshape(B, C, NAT_HEADS, NAT_HEAD_DIM)
        vnc = vnc.reshape(B, C, NAT_HEADS, NAT_HEAD_DIM)
        o_c = neighbourhood_attention(qn.reshape(B, L, NAT_HEADS, NAT_HEAD_DIM),
                                      kn.reshape(B, L, NAT_HEADS, NAT_HEAD_DIM),
                                      vn.reshape(B, L, NAT_HEADS, NAT_HEAD_DIM),
                                      knc, vnc, nat_rpb[l]).reshape(B, L, NAT_WIDTH)

        x = x + ga1 * merge_branches(o_a, o_b, o_c, gl, w_branch[l], w_out[l])

        if not last:
            o_ac = diff_head_out(diff_attention(qac.reshape(B, C, H, 2, dh), kac, vac, lam), g_subln[l], lam_init)
            o_bc = multiscale_pool(puc, pool_w[l], pool_scale[l])
            o_cc = dense_attention(qnc.reshape(B, C, NAT_HEADS, NAT_HEAD_DIM), knc, vnc).reshape(B, C, NAT_WIDTH)
            xc = xc + cga1 * merge_branches(o_ac, o_bc, o_cc, glc, w_branch[l], w_out[l])

        h = modulate(rms_norm(x, g_mlp[l]), sh2, sc2)
        x = x + ga2 * sq_relu_mlp(h, w_mlp1[l], w_mlp2[l])
        if not last:
            hc = modulate(rms_norm(xc, g_mlp[l]), csh2, csc2)
            xc = xc + cga2 * sq_relu_mlp(hc, w_mlp1[l], w_mlp2[l])

    return rms_norm(x, g_final)
```

```python
import functools
import math

import numpy as np
import jax
import jax.numpy as jnp
from jax import lax
from jax.experimental import pallas as pl
from jax.experimental.pallas import tpu as pltpu

F32 = jnp.float32
BF16 = jnp.bfloat16

D = 1024
B = 8
L = 2048
C = 256
T = L + C
DEPTH = 4
GRID_W = 64
ROWS = L // GRID_W
DH = 64
DIFF_HEADS = 4
NAT_HEADS = 8
BW = 512
WIN_ROWS = 8
WIN_COLS = 16
N_LOC = WIN_ROWS * GRID_W
POOL_WINDOWS = (2, 4, 8, 16)
D_FF = 4 * D
IN_COLS = 7 * BW + 3 * D
ROPE_BASE = 10000.0
EPS = 1e-6
NEG = -1e30

LANES = 128
VMEM_LIMIT = 56 * 1024 * 1024

RC = 256
NCH = T // RC
TN = 512
NJ = IN_COLS // TN
KC = 768
TM = 768
TM_LAST = 1024
FC = 512

Z_GATE, Z_QA, Z_KA, Z_VA, Z_PU, Z_QN, Z_KN, Z_VN = 0, 6, 7, 8, 9, 10, 11, 12


def _cparams(n_axes):
    return pltpu.CompilerParams(dimension_semantics=("arbitrary",) * n_axes,
                                vmem_limit_bytes=VMEM_LIMIT)


def _dot(a, b):
    return jnp.dot(a, b, preferred_element_type=F32)


def _dot_nt(a, b):
    return lax.dot_general(a, b, (((1,), (1,)), ((), ())), preferred_element_type=F32)


ADA_TN = 1536


def _adaln_kernel(c_ref, w_ref, b_ref, o_ref):
    cond = c_ref[...]
    s = (cond * jax.nn.sigmoid(cond)).astype(BF16)
    o_ref[0] = _dot(s, w_ref[0].astype(BF16)) + b_ref[0]


def _adaln(cond, w_ada, b_ada):
    n = cond.shape[0]
    return pl.pallas_call(
        _adaln_kernel,
        grid=(DEPTH, 6 * D // ADA_TN),
        in_specs=[
            pl.BlockSpec((n, D), lambda l, j: (0, 0)),
            pl.BlockSpec((1, D, ADA_TN), lambda l, j: (l, 0, j)),
            pl.BlockSpec((1, 1, ADA_TN), lambda l, j: (l, 0, j)),
        ],
        out_specs=pl.BlockSpec((1, n, ADA_TN), lambda l, j: (l, 0, j)),
        out_shape=jax.ShapeDtypeStruct((DEPTH, n, 6 * D), F32),
        compiler_params=_cparams(2),
        name="adaln",
    )(cond, w_ada, b_ada.reshape(DEPTH, 1, 6 * D))


def _inproj_kernel(x_ref, mod_ref, g_ref, w_ref, cos_ref, sin_ref, o_ref, h_s):
    j = pl.program_id(1)

    @pl.when(j == 0)
    def _():
        g = g_ref[...]
        for r in range(NCH):
            seg = 1 if r == NCH - 1 else 0
            shift = mod_ref[0, seg, 0:1, :]
            scale = mod_ref[0, seg, 1:2, :]
            xr = x_ref[0, r * RC:(r + 1) * RC, :]
            ms = jnp.mean(xr * xr, axis=-1, keepdims=True)
            y = xr * lax.rsqrt(ms + EPS) * g
            h_s[r * RC:(r + 1) * RC, :] = (y * (1.0 + scale) + shift).astype(BF16)

    def run(epilogue):
        for r in range(NCH):
            rows = slice(r * RC, (r + 1) * RC)
            acc = _dot(h_s[rows, :], w_ref[...])
            epilogue(acc, rows)

    def plain(acc, rows):
        o_ref[0, rows, :] = acc.astype(BF16)

    def scaled(acc, rows):
        o_ref[0, rows, :] = (acc * DH ** -0.5).astype(BF16)

    def gate(acc, rows):
        o_ref[0, rows, :] = jax.nn.sigmoid(acc).astype(BF16)

    def rope(pre):
        def epilogue(acc, rows):
            cs = cos_ref[rows, :]
            sn = sin_ref[rows, :]
            upper = (lax.broadcasted_iota(jnp.int32, (RC, LANES), 1) & 16) != 0
            for cc in range(TN // LANES):
                u = acc[:, cc * LANES:(cc + 1) * LANES]
                if pre != 1.0:
                    u = u * pre
                partner = jnp.where(upper, pltpu.roll(u, 16, 1), pltpu.roll(u, LANES - 16, 1))
                o_ref[0, rows, cc * LANES:(cc + 1) * LANES] = (u * cs + partner * sn).astype(BF16)
        return epilogue

    pl.when(j == 0)(lambda: run(rope(DH ** -0.5)))
    pl.when(j == 1)(lambda: run(rope(1.0)))
    pl.when(j == 4)(lambda: run(scaled))
    pl.when(j >= 7)(lambda: run(gate))
    pl.when((j == 2) | (j == 3) | (j == 5) | (j == 6))(lambda: run(plain))


def _inproj(xs, modb, g, w, cos_t, sin_t):
    return pl.pallas_call(
        _inproj_kernel,
        grid=(B, NJ),
        in_specs=[
            pl.BlockSpec((1, T, D), lambda b, j: (b, 0, 0)),
            pl.BlockSpec((1, 2, 6, D), lambda b, j: (b, 0, 0, 0)),
            pl.BlockSpec((1, D), lambda b, j: (0, 0)),
            pl.BlockSpec((D, TN), lambda b, j: (0, j)),
            pl.BlockSpec((T, LANES), lambda b, j: (0, 0)),
            pl.BlockSpec((T, LANES), lambda b, j: (0, 0)),
        ],
        out_specs=pl.BlockSpec((1, T, TN), lambda b, j: (b, 0, (j + 6) % NJ)),
        out_shape=jax.ShapeDtypeStruct((B, T, IN_COLS), BF16),
        scratch_shapes=[pltpu.VMEM((T, D), BF16)],
        compiler_params=_cparams(2),
        name="inproj",
    )(xs, modb, g, w, cos_t, sin_t)


def _rope_tables():
    nf = DH // 4
    pos = jnp.arange(L)
    row = (pos // GRID_W).astype(F32)
    col = (pos % GRID_W).astype(F32)
    inv = ROPE_BASE ** (-jnp.arange(nf, dtype=F32) / nf)

    def part(p):
        ang = p[:, None] * inv
        cs = jnp.cos(ang).astype(F32)
        sn = jnp.sin(ang).astype(F32)
        return jnp.concatenate([cs, cs], axis=1), jnp.concatenate([-sn, sn], axis=1)

    cr, sr = part(row)
    cc, sc = part(col)
    cos_h = jnp.concatenate([cr, cc], axis=1)
    sin_h = jnp.concatenate([sr, sc], axis=1)
    reps = LANES // DH
    cos_t = jnp.concatenate([jnp.tile(cos_h, (1, reps)), jnp.ones((C, LANES), F32)], axis=0)
    sin_t = jnp.concatenate([jnp.tile(sin_h, (1, reps)), jnp.zeros((C, LANES), F32)], axis=0)
    return cos_t, sin_t


def _split_heads(k_ref, lo_s, hi_s):
    kf = k_ref[0].astype(F32)
    lane = lax.broadcasted_iota(jnp.int32, (T, LANES), 1)
    lo_s[...] = jnp.where(lane < DH, kf, 0.0).astype(BF16)
    hi_s[...] = jnp.where(lane >= DH, kf, 0.0).astype(BF16)


def _extend_values(v_ref, vx_s):
    vx_s[:, :LANES] = v_ref[0]
    vx_s[:, LANES:] = jnp.ones((T, LANES), BF16)


def _diff_kernel(lam0_ref, q_ref, k_ref, v_ref, lp_ref, gs_ref, o_ref, k0_s, k1_s, vx_s, s0_s, s1_s):
    i = pl.program_id(2)

    @pl.when(i == 0)
    def _():
        _split_heads(k_ref, k0_s, k1_s)
        _extend_values(v_ref, vx_s)

    lam_init = lam0_ref[0]
    lp = lp_ref[...]
    lam = (jnp.exp(jnp.sum(lp[0:1] * lp[1:2], axis=-1, keepdims=True))
           - jnp.exp(jnp.sum(lp[2:3] * lp[3:4], axis=-1, keepdims=True)) + lam_init)

    def attend(k_lo, k_hi, kc):
        q = q_ref[0]
        heads = []
        for ks, ss in ((k0_s, s0_s), (k1_s, s1_s)):
            m = None
            for c0 in range(k_lo, k_hi, kc):
                s = _dot_nt(q, ks[c0:c0 + kc, :])
                ss[:, c0 - k_lo:c0 - k_lo + kc] = s
                mc = jnp.max(s, axis=-1, keepdims=True)
                m = mc if m is None else jnp.maximum(m, mc)
            acc = None
            for c0 in range(k_lo, k_hi, kc):
                e = jnp.exp(ss[:, c0 - k_lo:c0 - k_lo + kc] - m).astype(BF16)
                part = _dot(e, vx_s[c0:c0 + kc, :])
                acc = part if acc is None else acc + part
            heads.append(acc[:, :LANES] / acc[:, LANES:])
        o = heads[0] - lam * heads[1]
        ms = jnp.mean(o * o, axis=-1, keepdims=True)
        y = o * lax.rsqrt(ms + EPS) * gs_ref[...]
        o_ref[0] = (y * (1.0 - lam_init)).astype(BF16)

    pl.when(i < NCH - 1)(lambda: attend(0, T, KC))
    pl.when(i == NCH - 1)(lambda: attend(L, T, C))


def _diff_attention(z, lam0, lparams, g_subln):
    qb = Z_QA * TN // LANES
    kb = Z_KA * TN // LANES
    vb = Z_VA * TN // LANES
    return pl.pallas_call(
        _diff_kernel,
        grid=(B, DIFF_HEADS, NCH),
        in_specs=[
            pl.BlockSpec(memory_space=pltpu.SMEM),
            pl.BlockSpec((1, RC, LANES), lambda b, h, i: (b, i, qb + h)),
            pl.BlockSpec((1, T, LANES), lambda b, h, i: (b, 0, kb + h)),
            pl.BlockSpec((1, T, LANES), lambda b, h, i: (b, 0, vb + h)),
            pl.BlockSpec((4, DH), lambda b, h, i: (0, 0)),
            pl.BlockSpec((1, LANES), lambda b, h, i: (0, 0)),
        ],
        out_specs=pl.BlockSpec((1, RC, LANES), lambda b, h, i: (b, i, h)),
        out_shape=jax.ShapeDtypeStruct((B, T, BW), BF16),
        scratch_shapes=[
            pltpu.VMEM((T, LANES), BF16),
            pltpu.VMEM((T, LANES), BF16),
            pltpu.VMEM((T, 2 * LANES), BF16),
            pltpu.VMEM((RC, T), F32),
            pltpu.VMEM((RC, T), F32),
        ],
        compiler_params=_cparams(3),
        name="diff_attention",
    )(lam0, z, z, z, lparams, g_subln)


def _nat_kernel(q_ref, k_ref, v_ref, bias_ref, o_ref, k0_s, k1_s, vx_s):
    _split_heads(k_ref, k0_s, k1_s)
    _extend_values(v_ref, vx_s)
    ctx = slice(L, T)
    low = lax.broadcasted_iota(jnp.int32, (C, LANES), 1) < DH

    qc = q_ref[0, ctx, :]
    outs = []
    for ks in (k0_s, k1_s):
        s = _dot_nt(qc, ks[ctx, :])
        e = jnp.exp(s - jnp.max(s, axis=-1, keepdims=True)).astype(BF16)
        acc = _dot(e, vx_s[ctx, :])
        outs.append(acc[:, :LANES] / acc[:, LANES:])
    o_ref[0, ctx, :] = jnp.where(low, outs[0], outs[1]).astype(BF16)

    low_r = lax.broadcasted_iota(jnp.int32, (GRID_W, LANES), 1) < DH

    def row(r, carry):
        rs = jnp.clip(r - WIN_ROWS // 2, 0, ROWS - WIN_ROWS)
        cls = r - rs
        q_rows = pl.ds(pl.multiple_of(r * GRID_W, GRID_W), GRID_W)
        k_rows = pl.ds(pl.multiple_of(rs * GRID_W, GRID_W), N_LOC)
        q = q_ref[0, q_rows, :]
        res = []
        for hh, ks in enumerate((k0_s, k1_s)):
            s_loc = _dot_nt(q, ks[k_rows, :]) + bias_ref[hh, cls]
            s_ctx = _dot_nt(q, ks[ctx, :])
            m = jnp.maximum(jnp.max(s_loc, axis=-1, keepdims=True),
                            jnp.max(s_ctx, axis=-1, keepdims=True))
            e_loc = jnp.exp(s_loc - m).astype(BF16)
            e_ctx = jnp.exp(s_ctx - m).astype(BF16)
            acc = _dot(e_loc, vx_s[k_rows, :]) + _dot(e_ctx, vx_s[ctx, :])
            res.append(acc[:, :LANES] / acc[:, LANES:])
        o_ref[0, q_rows, :] = jnp.where(low_r, res[0], res[1]).astype(BF16)
        return carry

    lax.fori_loop(0, ROWS, row, 0)


def _nat_bias_table(rpb):
    cls = np.arange(WIN_ROWS)
    jj = np.arange(WIN_ROWS)
    w = np.arange(GRID_W)
    cc = np.arange(GRID_W)
    d_row = jj[None, :] - cls[:, None] + (WIN_ROWS - 1)
    start = np.clip(w - WIN_COLS // 2, 0, GRID_W - WIN_COLS)
    inside = (cc[None, :] >= start[:, None]) & (cc[None, :] < start[:, None] + WIN_COLS)
    d_col = np.clip(cc[None, :] - w[:, None] + (WIN_COLS - 1), 0, 2 * WIN_COLS - 2)
    tab = rpb.astype(F32)[:, d_row[:, :, None, None], d_col[None, None, :, :]]
    tab = jnp.where(inside[None, None, None], tab, NEG)
    return tab.transpose(0, 1, 3, 2, 4).reshape(NAT_HEADS, WIN_ROWS, GRID_W, N_LOC)


def _nat_attention(z, bias):
    qb = Z_QN * TN // LANES
    kb = Z_KN * TN // LANES
    vb = Z_VN * TN // LANES
    return pl.pallas_call(
        _nat_kernel,
        grid=(B, NAT_HEADS // 2),
        in_specs=[
            pl.BlockSpec((1, T, LANES), lambda b, p: (b, 0, qb + p)),
            pl.BlockSpec((1, T, LANES), lambda b, p: (b, 0, kb + p)),
            pl.BlockSpec((1, T, LANES), lambda b, p: (b, 0, vb + p)),
            pl.BlockSpec((2, WIN_ROWS, GRID_W, N_LOC), lambda b, p: (p, 0, 0, 0)),
        ],
        out_specs=pl.BlockSpec((1, T, LANES), lambda b, p: (b, 0, p)),
        out_shape=jax.ShapeDtypeStruct((B, T, BW), BF16),
        scratch_shapes=[
            pltpu.VMEM((T, LANES), BF16),
            pltpu.VMEM((T, LANES), BF16),
            pltpu.VMEM((T, 2 * LANES), BF16),
        ],
        compiler_params=_cparams(2),
        name="nat_attention",
    )(z, z, z, bias)


def _pool_kernel(u_ref, w_ref, sc_ref, o_ref):
    for lo, n in ((0, L), (L, C)):
        t = lax.broadcasted_iota(jnp.int32, (n, LANES), 0)
        for gi, win in enumerate(POOL_WINDOWS):
            half = win // 2
            cols = slice(gi * LANES, (gi + 1) * LANES)
            u = u_ref[0, lo:lo + n, cols].astype(F32)

            def up(a, k):
                return jnp.where(t < n - k, pltpu.roll(a, n - k, 0), 0.0)

            def down(a, k):
                return jnp.where(t >= k, pltpu.roll(a, k, 0), 0.0)

            fwd = u
            bwd = down(u, 1)
            span = 1
            while span < half:
                fwd = fwd + up(fwd, span)
                bwd = bwd + down(bwd, span)
                span *= 2
            cnt = (jnp.minimum(t + half, n) - jnp.maximum(t - half, 0)).astype(F32)
            p = (fwd + bwd) / cnt - u
            y = _dot(p.astype(BF16), w_ref[gi]) * sc_ref[:, cols]
            o_ref[0, lo:lo + n, cols] = y.astype(BF16)


def _pool(z, pool_w, pool_scale):
    return pl.pallas_call(
        _pool_kernel,
        grid=(B,),
        in_specs=[
            pl.BlockSpec((1, T, BW), lambda b: (b, 0, Z_PU)),
            pl.BlockSpec((len(POOL_WINDOWS), LANES, LANES), lambda b: (0, 0, 0)),
            pl.BlockSpec((1, BW), lambda b: (0, 0)),
        ],
        out_specs=pl.BlockSpec((1, T, BW), lambda b: (b, 0, 0)),
        out_shape=jax.ShapeDtypeStruct((B, T, BW), BF16),
        compiler_params=_cparams(1),
        name="pool",
    )(z, pool_w, pool_scale)


def _mod_row(mod_ref, idx, is_ctx):
    lat = mod_ref[0, 0, idx:idx + 1, :]
    if is_ctx is None:
        return lat
    return jnp.where(is_ctx, mod_ref[0, 1, idx:idx + 1, :], lat)


def _merge_kernel(oa_ref, ob_ref, oc_ref, g_ref, x_ref, mod_ref, wb_ref, wo_ref, o_ref, *, tm, has_ctx):
    i = pl.program_id(1)
    nch = tm // RC
    for r in range(nch):
        rows = slice(r * RC, (r + 1) * RC)
        is_ctx = (i == T // tm - 1) if (has_ctx and r == nch - 1) else None
        merged = None
        for br, ref in enumerate((oa_ref, ob_ref, oc_ref)):
            gate = g_ref[0, rows, br * D:(br + 1) * D].astype(F32)
            term = gate * _dot(ref[0, rows, :], wb_ref[br])
            merged = term if merged is None else merged + term
        y = _dot(merged.astype(BF16), wo_ref[...])
        o_ref[0, rows, :] = x_ref[0, rows, :] + _mod_row(mod_ref, 2, is_ctx) * y


def _merge(oa, ob, oc, z, xs, modb, wb, wo, last):
    tm = TM_LAST if last else TM
    n_rows = L if last else T
    row_spec = lambda w: pl.BlockSpec((1, tm, w), lambda b, i: (b, i, 0))
    return pl.pallas_call(
        functools.partial(_merge_kernel, tm=tm, has_ctx=not last),
        grid=(B, n_rows // tm),
        in_specs=[
            row_spec(BW), row_spec(BW), row_spec(BW),
            pl.BlockSpec((1, tm, 3 * D), lambda b, i: (b, i, Z_GATE)),
            row_spec(D),
            pl.BlockSpec((1, 2, 6, D), lambda b, i: (b, 0, 0, 0)),
            pl.BlockSpec((3, BW, D), lambda b, i: (0, 0, 0)),
            pl.BlockSpec((D, D), lambda b, i: (0, 0)),
        ],
        out_specs=row_spec(D),
        out_shape=jax.ShapeDtypeStruct((B, n_rows, D), F32),
        compiler_params=_cparams(2),
        name="merge",
    )(oa, ob, oc, z, xs, modb, wb, wo)


def _mlp_kernel(x_ref, mod_ref, g_ref, w1_ref, w2_ref, gf_ref, o_ref, *, tm, has_ctx, final_norm):
    i = pl.program_id(1)
    nch = tm // RC
    g = g_ref[...]
    for r in range(nch):
        rows = slice(r * RC, (r + 1) * RC)
        is_ctx = (i == T // tm - 1) if (has_ctx and r == nch - 1) else None
        x = x_ref[0, rows, :]
        ms = jnp.mean(x * x, axis=-1, keepdims=True)
        y = x * lax.rsqrt(ms + EPS) * g
        h = (y * (1.0 + _mod_row(mod_ref, 4, is_ctx)) + _mod_row(mod_ref, 3, is_ctx)).astype(BF16)
        acc = None
        for f in range(D_FF // FC):
            cols = slice(f * FC, (f + 1) * FC)
            a = jnp.maximum(_dot(h, w1_ref[:, cols]), 0.0)
            part = _dot((a * a).astype(BF16), w2_ref[cols, :])
            acc = part if acc is None else acc + part
        out = x + _mod_row(mod_ref, 5, is_ctx) * acc
        if final_norm:
            ms = jnp.mean(out * out, axis=-1, keepdims=True)
            out = out * lax.rsqrt(ms + EPS) * gf_ref[...]
        o_ref[0, rows, :] = out


def _mlp(xs, modb, g, w1, w2, g_final, last):
    tm = TM_LAST if last else TM
    n_rows = xs.shape[1]
    row_spec = pl.BlockSpec((1, tm, D), lambda b, i: (b, i, 0))
    const = lambda shape: pl.BlockSpec(shape, lambda b, i: (0,) * len(shape), pipeline_mode=pl.Buffered(1))
    return pl.pallas_call(
        functools.partial(_mlp_kernel, tm=tm, has_ctx=not last, final_norm=last),
        grid=(B, n_rows // tm),
        in_specs=[
            row_spec,
            pl.BlockSpec((1, 2, 6, D), lambda b, i: (b, 0, 0, 0)),
            pl.BlockSpec((1, D), lambda b, i: (0, 0)),
            const((D, D_FF)),
            const((D_FF, D)),
            pl.BlockSpec((1, D), lambda b, i: (0, 0)),
        ],
        out_specs=row_spec,
        out_shape=jax.ShapeDtypeStruct((B, n_rows, D), F32),
        compiler_params=_cparams(2),
        name="mlp",
    )(xs, modb, g, w1, w2, g_final)


def kernel(x, c, ctx, c_ctx, w_ada, b_ada, g_mix, g_mlp, w_in, lambda_q1, lambda_k1, lambda_q2, lambda_k2,
           g_subln, pool_w, pool_scale, nat_rpb, w_branch, w_out, w_mlp1, w_mlp2, g_final):
    xs = jnp.concatenate([x, ctx], axis=1)
    pad = jnp.zeros((16 - B - 1, D), F32)
    cond = jnp.concatenate([c, c_ctx[None, :], pad], axis=0)
    mod = _adaln(cond, w_ada, b_ada)
    cos_t, sin_t = _rope_tables()

    for l in range(DEPTH):
        last = l == DEPTH - 1
        lam_init = 0.8 - 0.6 * math.exp(-0.3 * l)
        lat = mod[l, :B].reshape(B, 1, 6, D)
        cx = jnp.broadcast_to(mod[l, B].reshape(1, 1, 6, D), (B, 1, 6, D))
        modb = jnp.concatenate([lat, cx], axis=1)

        z = _inproj(xs, modb, g_mix[l][None, :], w_in[l].astype(BF16), cos_t, sin_t)
        lparams = jnp.stack([lambda_q1[l], lambda_k1[l], lambda_q2[l], lambda_k2[l]]).astype(F32)
        gs = g_subln[l].astype(F32)[None, :]
        oa = _diff_attention(z, jnp.full((1,), lam_init, F32), lparams, gs)
        ob = _pool(z, pool_w[l].astype(BF16), pool_scale[l][None, :])
        oc = _nat_attention(z, _nat_bias_table(nat_rpb[l]))
        xs = _merge(oa, ob, oc, z, xs, modb, w_branch[l].astype(BF16), w_out[l].astype(BF16), last)
        xs = _mlp(xs, modb, g_mlp[l][None, :], w_mlp1[l].astype(BF16), w_mlp2[l].astype(BF16),
                  g_final[None, :], last)
    return xs
```

```python
import functools
import math

import numpy as np
import jax
import jax.numpy as jnp
from jax import lax
from jax.experimental import pallas as pl
from jax.experimental.pallas import tpu as pltpu

F32 = jnp.float32
BF16 = jnp.bfloat16

D = 1024
B = 8
L = 2048
C = 256
T = L + C
DEPTH = 4
GRID_W = 64
ROWS = L // GRID_W
DH = 64
DIFF_HEADS = 4
NAT_HEADS = 8
BW = 512
WIN_ROWS = 8
WIN_COLS = 16
N_LOC = WIN_ROWS * GRID_W
POOL_WINDOWS = (2, 4, 8, 16)
D_FF = 4 * D
IN_COLS = 7 * BW + 3 * D
ROPE_BASE = 10000.0
EPS = 1e-6
NEG = -1e30

LANES = 128
VMEM_LIMIT = 56 * 1024 * 1024

RC = 256
NCH = T // RC
TN = 512
NJ = IN_COLS // TN
KC = 768
NAT_RB = 512
TM = 768
TM_LAST = 1024
FC = 512

Z_GATE, Z_QA, Z_KA, Z_VA, Z_PU, Z_QN, Z_KN, Z_VN = 0, 6, 7, 8, 9, 10, 11, 12


def _cparams(n_axes):
    return pltpu.CompilerParams(dimension_semantics=("arbitrary",) * n_axes,
                                vmem_limit_bytes=VMEM_LIMIT)


def _dot(a, b):
    return jnp.dot(a, b, preferred_element_type=F32)


def _dot_nt(a, b):
    return lax.dot_general(a, b, (((1,), (1,)), ((), ())), preferred_element_type=F32)


ADA_TN = 1536


def _adaln_kernel(c_ref, w_ref, b_ref, o_ref):
    cond = c_ref[...]
    s = (cond * jax.nn.sigmoid(cond)).astype(BF16)
    o_ref[0] = _dot(s, w_ref[0].astype(BF16)) + b_ref[0]


def _adaln(cond, w_ada, b_ada):
    n = cond.shape[0]
    return pl.pallas_call(
        _adaln_kernel,
        grid=(DEPTH, 6 * D // ADA_TN),
        in_specs=[
            pl.BlockSpec((n, D), lambda l, j: (0, 0)),
            pl.BlockSpec((1, D, ADA_TN), lambda l, j: (l, 0, j)),
            pl.BlockSpec((1, 1, ADA_TN), lambda l, j: (l, 0, j)),
        ],
        out_specs=pl.BlockSpec((1, n, ADA_TN), lambda l, j: (l, 0, j)),
        out_shape=jax.ShapeDtypeStruct((DEPTH, n, 6 * D), F32),
        compiler_params=_cparams(2),
        name="adaln",
    )(cond, w_ada, b_ada.reshape(DEPTH, 1, 6 * D))


def _inproj_kernel(x_ref, mod_ref, g_ref, w_ref, cos_ref, sin_ref, o_ref, h_s):
    j = pl.program_id(1)

    @pl.when(j == 0)
    def _():
        g = g_ref[...]
        for r in range(NCH):
            seg = 1 if r == NCH - 1 else 0
            shift = mod_ref[0, seg, 0:1, :]
            scale = mod_ref[0, seg, 1:2, :]
            xr = x_ref[0, r * RC:(r + 1) * RC, :]
            ms = jnp.mean(xr * xr, axis=-1, keepdims=True)
            y = xr * lax.rsqrt(ms + EPS) * g
            h_s[r * RC:(r + 1) * RC, :] = (y * (1.0 + scale) + shift).astype(BF16)

    def run(epilogue):
        for r in range(NCH):
            rows = slice(r * RC, (r + 1) * RC)
            acc = _dot(h_s[rows, :], w_ref[...])
            epilogue(acc, rows)

    def plain(acc, rows):
        o_ref[0, rows, :] = acc.astype(BF16)

    def scaled(acc, rows):
        o_ref[0, rows, :] = (acc * DH ** -0.5).astype(BF16)

    def gate(acc, rows):
        o_ref[0, rows, :] = jax.nn.sigmoid(acc).astype(BF16)

    def rope(pre):
        def epilogue(acc, rows):
            cs = cos_ref[rows, :]
            sn = sin_ref[rows, :]
            upper = (lax.broadcasted_iota(jnp.int32, (RC, LANES), 1) & 16) != 0
            for cc in range(TN // LANES):
                u = acc[:, cc * LANES:(cc + 1) * LANES]
                if pre != 1.0:
                    u = u * pre
                partner = jnp.where(upper, pltpu.roll(u, 16, 1), pltpu.roll(u, LANES - 16, 1))
                o_ref[0, rows, cc * LANES:(cc + 1) * LANES] = (u * cs + partner * sn).astype(BF16)
        return epilogue

    pl.when(j == 0)(lambda: run(rope(DH ** -0.5)))
    pl.when(j == 1)(lambda: run(rope(1.0)))
    pl.when(j == 4)(lambda: run(scaled))
    pl.when(j >= 7)(lambda: run(gate))
    pl.when((j == 2) | (j == 3) | (j == 5) | (j == 6))(lambda: run(plain))


def _inproj(xs, modb, g, w, cos_t, sin_t):
    return pl.pallas_call(
        _inproj_kernel,
        grid=(B, NJ),
        in_specs=[
            pl.BlockSpec((1, T, D), lambda b, j: (b, 0, 0)),
            pl.BlockSpec((1, 2, 6, D), lambda b, j: (b, 0, 0, 0)),
            pl.BlockSpec((1, D), lambda b, j: (0, 0)),
            pl.BlockSpec((D, TN), lambda b, j: (0, j)),
            pl.BlockSpec((T, LANES), lambda b, j: (0, 0)),
            pl.BlockSpec((T, LANES), lambda b, j: (0, 0)),
        ],
        out_specs=pl.BlockSpec((1, T, TN), lambda b, j: (b, 0, (j + 6) % NJ)),
        out_shape=jax.ShapeDtypeStruct((B, T, IN_COLS), BF16),
        scratch_shapes=[pltpu.VMEM((T, D), BF16)],
        compiler_params=_cparams(2),
        name="inproj",
    )(xs, modb, g, w, cos_t, sin_t)


def _rope_tables():
    nf = DH // 4
    pos = jnp.arange(L)
    row = (pos // GRID_W).astype(F32)
    col = (pos % GRID_W).astype(F32)
    inv = ROPE_BASE ** (-jnp.arange(nf, dtype=F32) / nf)

    def part(p):
        ang = p[:, None] * inv
        cs = jnp.cos(ang).astype(F32)
        sn = jnp.sin(ang).astype(F32)
        return jnp.concatenate([cs, cs], axis=1), jnp.concatenate([-sn, sn], axis=1)

    cr, sr = part(row)
    cc, sc = part(col)
    cos_h = jnp.concatenate([cr, cc], axis=1)
    sin_h = jnp.concatenate([sr, sc], axis=1)
    reps = LANES // DH
    cos_t = jnp.concatenate([jnp.tile(cos_h, (1, reps)), jnp.ones((C, LANES), F32)], axis=0)
    sin_t = jnp.concatenate([jnp.tile(sin_h, (1, reps)), jnp.zeros((C, LANES), F32)], axis=0)
    return cos_t, sin_t


def _split_heads(k_ref, lo_s, hi_s):
    kf = k_ref[0].astype(F32)
    lane = lax.broadcasted_iota(jnp.int32, (T, LANES), 1)
    lo_s[...] = jnp.where(lane < DH, kf, 0.0).astype(BF16)
    hi_s[...] = jnp.where(lane >= DH, kf, 0.0).astype(BF16)


def _extend_values(v_ref, vx_s):
    vx_s[:, :LANES] = v_ref[0]
    vx_s[:, LANES:] = jnp.ones((T, LANES), BF16)


def _diff_kernel(lam0_ref, q_ref, k_ref, v_ref, lp_ref, gs_ref, o_ref, k0_s, k1_s, vx_s, s0_s, s1_s):
    i = pl.program_id(2)

    @pl.when(i == 0)
    def _():
        _split_heads(k_ref, k0_s, k1_s)
        _extend_values(v_ref, vx_s)

    lam_init = lam0_ref[0]
    lp = lp_ref[...]
    lam = (jnp.exp(jnp.sum(lp[0:1] * lp[1:2], axis=-1, keepdims=True))
           - jnp.exp(jnp.sum(lp[2:3] * lp[3:4], axis=-1, keepdims=True)) + lam_init)

    def attend(k_lo, k_hi, kc):
        q = q_ref[0]
        maps = ((k0_s, s0_s), (k1_s, s1_s))
        m = [None, None]
        for c0 in range(k_lo, k_hi, kc):
            for mi, (ks, ss) in enumerate(maps):
                s = _dot_nt(q, ks[c0:c0 + kc, :])
                ss[:, c0 - k_lo:c0 - k_lo + kc] = s
                mc = jnp.max(s, axis=-1, keepdims=True)
                m[mi] = mc if m[mi] is None else jnp.maximum(m[mi], mc)
        acc = [None, None]
        for c0 in range(k_lo, k_hi, kc):
            for mi, (ks, ss) in enumerate(maps):
                e = jnp.exp(ss[:, c0 - k_lo:c0 - k_lo + kc] - m[mi]).astype(BF16)
                part = _dot(e, vx_s[c0:c0 + kc, :])
                acc[mi] = part if acc[mi] is None else acc[mi] + part
        heads = [a[:, :LANES] / a[:, LANES:] for a in acc]
        o = heads[0] - lam * heads[1]
        ms = jnp.mean(o * o, axis=-1, keepdims=True)
        y = o * lax.rsqrt(ms + EPS) * gs_ref[...]
        o_ref[0] = (y * (1.0 - lam_init)).astype(BF16)

    pl.when(i < NCH - 1)(lambda: attend(0, T, KC))
    pl.when(i == NCH - 1)(lambda: attend(L, T, C))


def _diff_attention(z, lam0, lparams, g_subln):
    qb = Z_QA * TN // LANES
    kb = Z_KA * TN // LANES
    vb = Z_VA * TN // LANES
    return pl.pallas_call(
        _diff_kernel,
        grid=(B, DIFF_HEADS, NCH),
        in_specs=[
            pl.BlockSpec(memory_space=pltpu.SMEM),
            pl.BlockSpec((1, RC, LANES), lambda b, h, i: (b, i, qb + h)),
            pl.BlockSpec((1, T, LANES), lambda b, h, i: (b, 0, kb + h)),
            pl.BlockSpec((1, T, LANES), lambda b, h, i: (b, 0, vb + h)),
            pl.BlockSpec((4, DH), lambda b, h, i: (0, 0)),
            pl.BlockSpec((1, LANES), lambda b, h, i: (0, 0)),
        ],
        out_specs=pl.BlockSpec((1, RC, LANES), lambda b, h, i: (b, i, h)),
        out_shape=jax.ShapeDtypeStruct((B, T, BW), BF16),
        scratch_shapes=[
            pltpu.VMEM((T, LANES), BF16),
            pltpu.VMEM((T, LANES), BF16),
            pltpu.VMEM((T, 2 * LANES), BF16),
            pltpu.VMEM((RC, T), F32),
            pltpu.VMEM((RC, T), F32),
        ],
        compiler_params=_cparams(3),
        name="diff_attention",
    )(lam0, z, z, z, lparams, g_subln)


def _nat_kernel(q_ref, k_ref, v_ref, bias_ref, o_ref, k0_s, k1_s, vx_s, sctx_s, ectx_s, acc_s):
    _split_heads(k_ref, k0_s, k1_s)
    _extend_values(v_ref, vx_s)
    ctx = slice(L, T)
    heads = (k0_s, k1_s)

    def finish(rows, accs):
        low = lax.broadcasted_iota(jnp.int32, accs[0][:, :LANES].shape, 1) < DH
        res = [a[:, :LANES] / a[:, LANES:] for a in accs]
        o_ref[0, rows, :] = jnp.where(low, res[0], res[1]).astype(BF16)

    qc = q_ref[0, ctx, :]
    accs = []
    for ks in heads:
        s = _dot_nt(qc, ks[ctx, :])
        e = jnp.exp(s - jnp.max(s, axis=-1, keepdims=True)).astype(BF16)
        accs.append(_dot(e, vx_s[ctx, :]))
    finish(ctx, accs)

    for hh, ks in enumerate(heads):
        for r0 in range(0, L, NAT_RB):
            sctx_s[hh, r0:r0 + NAT_RB, :] = _dot_nt(q_ref[0, r0:r0 + NAT_RB, :], ks[ctx, :])

    for r in range(ROWS):
        rs = min(max(r - WIN_ROWS // 2, 0), ROWS - WIN_ROWS)
        q_rows = slice(r * GRID_W, (r + 1) * GRID_W)
        k_rows = slice(rs * GRID_W, rs * GRID_W + N_LOC)
        q = q_ref[0, q_rows, :]
        for hh, ks in enumerate(heads):
            s_loc = _dot_nt(q, ks[k_rows, :]) + bias_ref[hh, r - rs]
            s_ctx = sctx_s[hh, q_rows, :]
            m = jnp.maximum(jnp.max(s_loc, axis=-1, keepdims=True),
                            jnp.max(s_ctx, axis=-1, keepdims=True))
            ectx_s[hh, q_rows, :] = jnp.exp(s_ctx - m).astype(BF16)
            acc_s[hh, q_rows, :] = _dot(jnp.exp(s_loc - m).astype(BF16), vx_s[k_rows, :])

    for r0 in range(0, L, NAT_RB):
        rows = slice(r0, r0 + NAT_RB)
        finish(rows, [acc_s[hh, rows, :] + _dot(ectx_s[hh, rows, :], vx_s[ctx, :]) for hh in range(2)])


def _nat_bias_table(rpb):
    n_dr, n_dc = 2 * WIN_ROWS - 1, 2 * WIN_COLS - 1
    w = np.arange(GRID_W)
    cc = np.arange(GRID_W)
    start = np.clip(w - WIN_COLS // 2, 0, GRID_W - WIN_COLS)
    inside = (cc[None, :] >= start[:, None]) & (cc[None, :] < start[:, None] + WIN_COLS)
    pad_lo = GRID_W - 1 - (WIN_COLS - 1)
    ext = jnp.pad(rpb.astype(F32), ((0, 0), (0, 0), (pad_lo, 2 * GRID_W - pad_lo - n_dc)))
    flat = jnp.tile(ext, (1, 1, GRID_W))[:, :, :GRID_W * (2 * GRID_W - 1)]
    toe = flat.reshape(NAT_HEADS, n_dr, GRID_W, 2 * GRID_W - 1)[:, :, :, GRID_W - 1:]
    toe = jnp.where(inside[None, None], toe, NEG)
    tab = jnp.stack([toe[:, WIN_ROWS - 1 - cls:2 * WIN_ROWS - 1 - cls] for cls in range(WIN_ROWS)], axis=1)
    return tab.transpose(0, 1, 3, 2, 4).reshape(NAT_HEADS, WIN_ROWS, GRID_W, N_LOC)


def _nat_attention(z, bias):
    qb = Z_QN * TN // LANES
    kb = Z_KN * TN // LANES
    vb = Z_VN * TN // LANES
    return pl.pallas_call(
        _nat_kernel,
        grid=(B, NAT_HEADS // 2),
        in_specs=[
            pl.BlockSpec((1, T, LANES), lambda b, p: (b, 0, qb + p)),
            pl.BlockSpec((1, T, LANES), lambda b, p: (b, 0, kb + p)),
            pl.BlockSpec((1, T, LANES), lambda b, p: (b, 0, vb + p)),
            pl.BlockSpec((2, WIN_ROWS, GRID_W, N_LOC), lambda b, p: (p, 0, 0, 0)),
        ],
        out_specs=pl.BlockSpec((1, T, LANES), lambda b, p: (b, 0, p)),
        out_shape=jax.ShapeDtypeStruct((B, T, BW), BF16),
        scratch_shapes=[
            pltpu.VMEM((T, LANES), BF16),
            pltpu.VMEM((T, LANES), BF16),
            pltpu.VMEM((T, 2 * LANES), BF16),
            pltpu.VMEM((2, L, C), F32),
            pltpu.VMEM((2, L, C), BF16),
            pltpu.VMEM((2, L, 2 * LANES), F32),
        ],
        compiler_params=_cparams(2),
        name="nat_attention",
    )(z, z, z, bias)


def _pool_kernel(u_ref, w_ref, sc_ref, o_ref):
    for lo, n in ((0, L), (L, C)):
        t = lax.broadcasted_iota(jnp.int32, (n, LANES), 0)
        for gi, win in enumerate(POOL_WINDOWS):
            half = win // 2
            cols = slice(gi * LANES, (gi + 1) * LANES)
            u = u_ref[0, lo:lo + n, cols].astype(F32)

            def up(a, k):
                return jnp.where(t < n - k, pltpu.roll(a, n - k, 0), 0.0)

            def down(a, k):
                return jnp.where(t >= k, pltpu.roll(a, k, 0), 0.0)

            fwd = u
            bwd = down(u, 1)
            span = 1
            while span < half:
                fwd = fwd + up(fwd, span)
                bwd = bwd + down(bwd, span)
                span *= 2
            cnt = (jnp.minimum(t + half, n) - jnp.maximum(t - half, 0)).astype(F32)
            p = (fwd + bwd) / cnt - u
            y = _dot(p.astype(BF16), w_ref[gi]) * sc_ref[:, cols]
            o_ref[0, lo:lo + n, cols] = y.astype(BF16)


def _pool(z, pool_w, pool_scale):
    return pl.pallas_call(
        _pool_kernel,
        grid=(B,),
        in_specs=[
            pl.BlockSpec((1, T, BW), lambda b: (b, 0, Z_PU)),
            pl.BlockSpec((len(POOL_WINDOWS), LANES, LANES), lambda b: (0, 0, 0)),
            pl.BlockSpec((1, BW), lambda b: (0, 0)),
        ],
        out_specs=pl.BlockSpec((1, T, BW), lambda b: (b, 0, 0)),
        out_shape=jax.ShapeDtypeStruct((B, T, BW), BF16),
        compiler_params=_cparams(1),
        name="pool",
    )(z, pool_w, pool_scale)


def _mod_row(mod_ref, idx, is_ctx):
    lat = mod_ref[0, 0, idx:idx + 1, :]
    if is_ctx is None:
        return lat
    return jnp.where(is_ctx, mod_ref[0, 1, idx:idx + 1, :], lat)


def _merge_kernel(oa_ref, ob_ref, oc_ref, g_ref, x_ref, mod_ref, wb_ref, wo_ref, o_ref, *, tm, has_ctx):
    i = pl.program_id(1)
    nch = tm // RC
    for r in range(nch):
        rows = slice(r * RC, (r + 1) * RC)
        is_ctx = (i == T // tm - 1) if (has_ctx and r == nch - 1) else None
        merged = None
        for br, ref in enumerate((oa_ref, ob_ref, oc_ref)):
            gate = g_ref[0, rows, br * D:(br + 1) * D].astype(F32)
            term = gate * _dot(ref[0, rows, :], wb_ref[br])
            merged = term if merged is None else merged + term
        y = _dot(merged.astype(BF16), wo_ref[...])
        o_ref[0, rows, :] = x_ref[0, rows, :] + _mod_row(mod_ref, 2, is_ctx) * y


def _merge(oa, ob, oc, z, xs, modb, wb, wo, last):
    tm = TM_LAST if last else TM
    n_rows = L if last else T
    row_spec = lambda w: pl.BlockSpec((1, tm, w), lambda b, i: (b, i, 0))
    return pl.pallas_call(
        functools.partial(_merge_kernel, tm=tm, has_ctx=not last),
        grid=(B, n_rows // tm),
        in_specs=[
            row_spec(BW), row_spec(BW), row_spec(BW),
            pl.BlockSpec((1, tm, 3 * D), lambda b, i: (b, i, Z_GATE)),
            row_spec(D),
            pl.BlockSpec((1, 2, 6, D), lambda b, i: (b, 0, 0, 0)),
            pl.BlockSpec((3, BW, D), lambda b, i: (0, 0, 0)),
            pl.BlockSpec((D, D), lambda b, i: (0, 0)),
        ],
        out_specs=row_spec(D),
        out_shape=jax.ShapeDtypeStruct((B, n_rows, D), F32),
        compiler_params=_cparams(2),
        name="merge",
    )(oa, ob, oc, z, xs, modb, wb, wo)


def _mlp_kernel(x_ref, mod_ref, g_ref, w1_ref, w2_ref, gf_ref, o_ref, *, tm, has_ctx, final_norm):
    i = pl.program_id(1)
    nch = tm // RC
    g = g_ref[...]
    for r in range(nch):
        rows = slice(r * RC, (r + 1) * RC)
        is_ctx = (i == T // tm - 1) if (has_ctx and r == nch - 1) else None
        x = x_ref[0, rows, :]
        ms = jnp.mean(x * x, axis=-1, keepdims=True)
        y = x * lax.rsqrt(ms + EPS) * g
        h = (y * (1.0 + _mod_row(mod_ref, 4, is_ctx)) + _mod_row(mod_ref, 3, is_ctx)).astype(BF16)
        acc = None
        for f in range(D_FF // FC):
            cols = slice(f * FC, (f + 1) * FC)
            a = jnp.maximum(_dot(h, w1_ref[:, cols]), 0.0)
            part = _dot((a * a).astype(BF16), w2_ref[cols, :])
            acc = part if acc is None else acc + part
        out = x + _mod_row(mod_ref, 5, is_ctx) * acc
        if final_norm:
            ms = jnp.mean(out * out, axis=-1, keepdims=True)
            out = out * lax.rsqrt(ms + EPS) * gf_ref[...]
        o_ref[0, rows, :] = out


def _mlp(xs, modb, g, w1, w2, g_final, last):
    tm = TM_LAST if last else TM
    n_rows = xs.shape[1]
    row_spec = pl.BlockSpec((1, tm, D), lambda b, i: (b, i, 0))
    const = lambda shape: pl.BlockSpec(shape, lambda b, i: (0,) * len(shape), pipeline_mode=pl.Buffered(1))
    return pl.pallas_call(
        functools.partial(_mlp_kernel, tm=tm, has_ctx=not last, final_norm=last),
        grid=(B, n_rows // tm),
        in_specs=[
            row_spec,
            pl.BlockSpec((1, 2, 6, D), lambda b, i: (b, 0, 0, 0)),
            pl.BlockSpec((1, D), lambda b, i: (0, 0)),
            const((D, D_FF)),
            const((D_FF, D)),
            pl.BlockSpec((1, D), lambda b, i: (0, 0)),
        ],
        out_specs=row_spec,
        out_shape=jax.ShapeDtypeStruct((B, n_rows, D), F32),
        compiler_params=_cparams(2),
        name="mlp",
    )(xs, modb, g, w1, w2, g_final)


def kernel(x, c, ctx, c_ctx, w_ada, b_ada, g_mix, g_mlp, w_in, lambda_q1, lambda_k1, lambda_q2, lambda_k2,
           g_subln, pool_w, pool_scale, nat_rpb, w_branch, w_out, w_mlp1, w_mlp2, g_final):
    xs = jnp.concatenate([x, ctx], axis=1)
    pad = jnp.zeros((16 - B - 1, D), F32)
    cond = jnp.concatenate([c, c_ctx[None, :], pad], axis=0)
    mod = _adaln(cond, w_ada, b_ada)
    cos_t, sin_t = _rope_tables()

    for l in range(DEPTH):
        last = l == DEPTH - 1
        lam_init = 0.8 - 0.6 * math.exp(-0.3 * l)
        lat = mod[l, :B].reshape(B, 1, 6, D)
        cx = jnp.broadcast_to(mod[l, B].reshape(1, 1, 6, D), (B, 1, 6, D))
        modb = jnp.concatenate([lat, cx], axis=1)

        z = _inproj(xs, modb, g_mix[l][None, :], w_in[l].astype(BF16), cos_t, sin_t)
        lparams = jnp.stack([lambda_q1[l], lambda_k1[l], lambda_q2[l], lambda_k2[l]]).astype(F32)
        gs = g_subln[l].astype(F32)[None, :]
        oa = _diff_attention(z, jnp.full((1,), lam_init, F32), lparams, gs)
        ob = _pool(z, pool_w[l].astype(BF16), pool_scale[l][None, :])
        oc = _nat_attention(z, _nat_bias_table(nat_rpb[l]))
        xs = _merge(oa, ob, oc, z, xs, modb, w_branch[l].astype(BF16), w_out[l].astype(BF16), last)
        xs = _mlp(xs, modb, g_mlp[l][None, :], w_mlp1[l].astype(BF16), w_mlp2[l].astype(BF16),
                  g_final[None, :], last)
    return xs
```

```python
import functools
import math

import numpy as np
import jax
import jax.numpy as jnp
from jax import lax
from jax.experimental import pallas as pl
from jax.experimental.pallas import tpu as pltpu

F32 = jnp.float32
BF16 = jnp.bfloat16

D = 1024
B = 8
L = 2048
C = 256
T = L + C
DEPTH = 4
GRID_W = 64
ROWS = L // GRID_W
DH = 64
DIFF_HEADS = 4
NAT_HEADS = 8
BW = 512
WIN_ROWS = 8
WIN_COLS = 16
N_LOC = WIN_ROWS * GRID_W
POOL_WINDOWS = (2, 4, 8, 16)
D_FF = 4 * D
IN_COLS = 7 * BW + 3 * D
ROPE_BASE = 10000.0
EPS = 1e-6
NEG = -1e30
LOG2E = math.log2(math.e)
Q_SCALE = DH ** -0.5 * LOG2E

LANES = 128
VMEM_LIMIT = 56 * 1024 * 1024

RC = 256
NCH = T // RC
TN = 512
NJ = IN_COLS // TN
KC = 768
NAT_RB = 512
TM = 768
TM_LAST = 1024
FC = 512

Z_GATE, Z_QA, Z_KA, Z_VA, Z_PU, Z_QN, Z_KN, Z_VN = 0, 6, 7, 8, 9, 10, 11, 12


def _cparams(n_axes):
    return pltpu.CompilerParams(dimension_semantics=("arbitrary",) * n_axes,
                                vmem_limit_bytes=VMEM_LIMIT)


def _dot(a, b):
    return jnp.dot(a, b, preferred_element_type=F32)


ADA_TN = 1536


def _adaln_kernel(c_ref, w_ref, b_ref, o_ref):
    cond = c_ref[...]
    s = (cond * jax.nn.sigmoid(cond)).astype(BF16)
    o_ref[0] = _dot(s, w_ref[0].astype(BF16)) + b_ref[0]


def _adaln(cond, w_ada, b_ada):
    n = cond.shape[0]
    return pl.pallas_call(
        _adaln_kernel,
        grid=(DEPTH, 6 * D // ADA_TN),
        in_specs=[
            pl.BlockSpec((n, D), lambda l, j: (0, 0)),
            pl.BlockSpec((1, D, ADA_TN), lambda l, j: (l, 0, j)),
            pl.BlockSpec((1, 1, ADA_TN), lambda l, j: (l, 0, j)),
        ],
        out_specs=pl.BlockSpec((1, n, ADA_TN), lambda l, j: (l, 0, j)),
        out_shape=jax.ShapeDtypeStruct((DEPTH, n, 6 * D), F32),
        compiler_params=_cparams(2),
        name="adaln",
    )(cond, w_ada, b_ada.reshape(DEPTH, 1, 6 * D))


def _inproj_kernel(x_ref, mod_ref, g_ref, w_ref, cos_ref, sin_ref, o_ref, h_s, w_s):
    j = pl.program_id(1)
    w_s[...] = w_ref[0].astype(BF16)

    @pl.when(j == 0)
    def _():
        g = g_ref[...]
        for r in range(NCH):
            seg = 1 if r == NCH - 1 else 0
            shift = mod_ref[0, seg, 0:1, :]
            scale = mod_ref[0, seg, 1:2, :]
            xr = x_ref[0, r * RC:(r + 1) * RC, :]
            ms = jnp.mean(xr * xr, axis=-1, keepdims=True)
            y = xr * lax.rsqrt(ms + EPS) * g
            h_s[r * RC:(r + 1) * RC, :] = (y * (1.0 + scale) + shift).astype(BF16)

    def run(epilogue):
        for r in range(NCH):
            rows = slice(r * RC, (r + 1) * RC)
            acc = _dot(h_s[rows, :], w_s[...])
            epilogue(acc, rows)

    def plain(acc, rows):
        o_ref[0, rows, :] = acc.astype(BF16)

    def scaled(acc, rows):
        o_ref[0, rows, :] = (acc * Q_SCALE).astype(BF16)

    def gate(acc, rows):
        o_ref[0, rows, :] = jax.nn.sigmoid(acc).astype(BF16)

    def rope(pre):
        def epilogue(acc, rows):
            cs = cos_ref[rows, :]
            sn = sin_ref[rows, :]
            upper = (lax.broadcasted_iota(jnp.int32, (RC, LANES), 1) & 16) != 0
            for cc in range(TN // LANES):
                u = acc[:, cc * LANES:(cc + 1) * LANES]
                if pre != 1.0:
                    u = u * pre
                partner = jnp.where(upper, pltpu.roll(u, 16, 1), pltpu.roll(u, LANES - 16, 1))
                o_ref[0, rows, cc * LANES:(cc + 1) * LANES] = (u * cs + partner * sn).astype(BF16)
        return epilogue

    pl.when(j == 0)(lambda: run(rope(Q_SCALE)))
    pl.when(j == 1)(lambda: run(rope(1.0)))
    pl.when(j == 4)(lambda: run(scaled))
    pl.when(j >= 7)(lambda: run(gate))
    pl.when((j == 2) | (j == 3) | (j == 5) | (j == 6))(lambda: run(plain))


def _inproj(xs, modb, g, w_in, layer, cos_t, sin_t):
    return pl.pallas_call(
        _inproj_kernel,
        grid=(B, NJ),
        in_specs=[
            pl.BlockSpec((1, T, D), lambda b, j: (b, 0, 0)),
            pl.BlockSpec((1, 2, 6, D), lambda b, j: (b, 0, 0, 0)),
            pl.BlockSpec((1, D), lambda b, j: (0, 0)),
            pl.BlockSpec((1, D, TN), lambda b, j: (layer, 0, j)),
            pl.BlockSpec((T, LANES), lambda b, j: (0, 0)),
            pl.BlockSpec((T, LANES), lambda b, j: (0, 0)),
        ],
        out_specs=pl.BlockSpec((1, T, TN), lambda b, j: (b, 0, (j + 6) % NJ)),
        out_shape=jax.ShapeDtypeStruct((B, T, IN_COLS), BF16),
        scratch_shapes=[pltpu.VMEM((T, D), BF16), pltpu.VMEM((D, TN), BF16)],
        compiler_params=_cparams(2),
        name="inproj",
    )(xs, modb, g, w_in, cos_t, sin_t)


def _rope_tables():
    nf = DH // 4
    pos = jnp.arange(L)
    row = (pos // GRID_W).astype(F32)
    col = (pos % GRID_W).astype(F32)
    inv = ROPE_BASE ** (-jnp.arange(nf, dtype=F32) / nf)

    def part(p):
        ang = p[:, None] * inv
        cs = jnp.cos(ang).astype(F32)
        sn = jnp.sin(ang).astype(F32)
        return jnp.concatenate([cs, cs], axis=1), jnp.concatenate([-sn, sn], axis=1)

    cr, sr = part(row)
    cc, sc = part(col)
    cos_h = jnp.concatenate([cr, cc], axis=1)
    sin_h = jnp.concatenate([sr, sc], axis=1)
    reps = LANES // DH
    cos_t = jnp.concatenate([jnp.tile(cos_h, (1, reps)), jnp.ones((C, LANES), F32)], axis=0)
    sin_t = jnp.concatenate([jnp.tile(sin_h, (1, reps)), jnp.zeros((C, LANES), F32)], axis=0)
    return cos_t, sin_t


def _split_heads(k_ref, lo_s, hi_s):
    kt = k_ref[0].astype(F32).T
    row = lax.broadcasted_iota(jnp.int32, (LANES, T), 0)
    lo_s[...] = jnp.where(row < DH, kt, 0.0).astype(BF16)
    hi_s[...] = jnp.where(row >= DH, kt, 0.0).astype(BF16)


def _extend_values(v_ref, vx_s):
    vx_s[:, :LANES] = v_ref[0]
    vx_s[:, LANES:] = jnp.ones((T, LANES), BF16)


def _diff_kernel(lam0_ref, q_ref, k_ref, v_ref, lp_ref, gs_ref, o_ref, k0_s, k1_s, vx_s, s0_s, s1_s):
    i = pl.program_id(2)

    @pl.when(i == 0)
    def _():
        _split_heads(k_ref, k0_s, k1_s)
        _extend_values(v_ref, vx_s)

    lam_init = lam0_ref[0]
    lp = lp_ref[...]
    lam = (jnp.exp(jnp.sum(lp[0:1] * lp[1:2], axis=-1, keepdims=True))
           - jnp.exp(jnp.sum(lp[2:3] * lp[3:4], axis=-1, keepdims=True)) + lam_init)

    def attend(k_lo, k_hi, kc):
        q = q_ref[0]
        maps = ((k0_s, s0_s), (k1_s, s1_s))
        m = [None, None]
        for c0 in range(k_lo, k_hi, kc):
            for mi, (ks, ss) in enumerate(maps):
                s = _dot(q, ks[:, c0:c0 + kc])
                ss[:, c0 - k_lo:c0 - k_lo + kc] = s
                mc = jnp.max(s, axis=-1, keepdims=True)
                m[mi] = mc if m[mi] is None else jnp.maximum(m[mi], mc)
        acc = [None, None]
        for c0 in range(k_lo, k_hi, kc):
            for mi, (ks, ss) in enumerate(maps):
                e = jnp.exp2(ss[:, c0 - k_lo:c0 - k_lo + kc] - m[mi]).astype(BF16)
                part = _dot(e, vx_s[c0:c0 + kc, :])
                acc[mi] = part if acc[mi] is None else acc[mi] + part
        heads = [a[:, :LANES] / a[:, LANES:] for a in acc]
        o = heads[0] - lam * heads[1]
        ms = jnp.mean(o * o, axis=-1, keepdims=True)
        y = o * lax.rsqrt(ms + EPS) * gs_ref[...]
        o_ref[0] = (y * (1.0 - lam_init)).astype(BF16)

    pl.when(i < NCH - 1)(lambda: attend(0, T, KC))
    pl.when(i == NCH - 1)(lambda: attend(L, T, C))


def _diff_attention(z, lam0, lparams, g_subln):
    qb = Z_QA * TN // LANES
    kb = Z_KA * TN // LANES
    vb = Z_VA * TN // LANES
    return pl.pallas_call(
        _diff_kernel,
        grid=(B, DIFF_HEADS, NCH),
        in_specs=[
            pl.BlockSpec(memory_space=pltpu.SMEM),
            pl.BlockSpec((1, RC, LANES), lambda b, h, i: (b, i, qb + h)),
            pl.BlockSpec((1, T, LANES), lambda b, h, i: (b, 0, kb + h)),
            pl.BlockSpec((1, T, LANES), lambda b, h, i: (b, 0, vb + h)),
            pl.BlockSpec((4, DH), lambda b, h, i: (0, 0)),
            pl.BlockSpec((1, LANES), lambda b, h, i: (0, 0)),
        ],
        out_specs=pl.BlockSpec((1, RC, LANES), lambda b, h, i: (b, i, h)),
        out_shape=jax.ShapeDtypeStruct((B, T, BW), BF16),
        scratch_shapes=[
            pltpu.VMEM((LANES, T), BF16),
            pltpu.VMEM((LANES, T), BF16),
            pltpu.VMEM((T, 2 * LANES), BF16),
            pltpu.VMEM((RC, T), F32),
            pltpu.VMEM((RC, T), F32),
        ],
        compiler_params=_cparams(3),
        name="diff_attention",
    )(lam0, z, z, z, lparams, g_subln)


def _nat_kernel(q_ref, k_ref, v_ref, bias_ref, o_ref, kt_s, vx_s, qq_s, qc_s, sctx_s, ectx_s, acc_s):
    kt_s[...] = k_ref[0].astype(F32).T.astype(BF16)
    _extend_values(v_ref, vx_s)
    ctx = slice(L, T)
    blk = 2 * GRID_W

    def stack_heads(dst, n_blocks, src_lo, rows_per_block):
        for i in range(n_blocks):
            qf = q_ref[0, src_lo + i * rows_per_block:src_lo + (i + 1) * rows_per_block, :].astype(F32)
            low = lax.broadcasted_iota(jnp.int32, qf.shape, 1) < DH
            dst[2 * i * rows_per_block:(2 * i + 1) * rows_per_block, :] = jnp.where(low, qf, 0.0).astype(BF16)
            dst[(2 * i + 1) * rows_per_block:(2 * i + 2) * rows_per_block, :] = jnp.where(low, 0.0, qf).astype(BF16)

    def unstack(res, n):
        low = lax.broadcasted_iota(jnp.int32, (n, LANES), 1) < DH
        return jnp.where(low, res[:n], res[n:]).astype(BF16)

    stack_heads(qq_s, ROWS, 0, GRID_W)
    stack_heads(qc_s, 1, L, C)

    s = _dot(qc_s[...], kt_s[:, ctx])
    e = jnp.exp2(s - jnp.max(s, axis=-1, keepdims=True)).astype(BF16)
    acc = _dot(e, vx_s[ctx, :])
    o_ref[0, ctx, :] = unstack(acc[:, :LANES] / acc[:, LANES:], C)

    for r0 in range(0, 2 * L, NAT_RB):
        sctx_s[r0:r0 + NAT_RB, :] = _dot(qq_s[r0:r0 + NAT_RB, :], kt_s[:, ctx])

    for r in range(ROWS):
        rs = min(max(r - WIN_ROWS // 2, 0), ROWS - WIN_ROWS)
        rows = slice(r * blk, (r + 1) * blk)
        keys = slice(rs * GRID_W, rs * GRID_W + N_LOC)
        s_loc = _dot(qq_s[rows, :], kt_s[:, keys]) + bias_ref[0, r - rs]
        s_ctx = sctx_s[rows, :]
        m = jnp.maximum(jnp.max(s_loc, axis=-1, keepdims=True), jnp.max(s_ctx, axis=-1, keepdims=True))
        ectx_s[rows, :] = jnp.exp2(s_ctx - m).astype(BF16)
        acc_s[rows, :] = _dot(jnp.exp2(s_loc - m).astype(BF16), vx_s[keys, :])

    for r0 in range(0, 2 * L, NAT_RB):
        rows = slice(r0, r0 + NAT_RB)
        acc = acc_s[rows, :] + _dot(ectx_s[rows, :], vx_s[ctx, :])
        res = acc[:, :LANES] / acc[:, LANES:]
        for i in range(NAT_RB // blk):
            r = r0 // blk + i
            o_ref[0, r * GRID_W:(r + 1) * GRID_W, :] = unstack(res[i * blk:(i + 1) * blk], GRID_W)


def _nat_bias_table(rpb):
    n_dr, n_dc = 2 * WIN_ROWS - 1, 2 * WIN_COLS - 1
    w = np.arange(GRID_W)
    cc = np.arange(GRID_W)
    start = np.clip(w - WIN_COLS // 2, 0, GRID_W - WIN_COLS)
    inside = (cc[None, :] >= start[:, None]) & (cc[None, :] < start[:, None] + WIN_COLS)
    pad_lo = GRID_W - 1 - (WIN_COLS - 1)
    ext = jnp.pad(rpb.astype(F32), ((0, 0), (0, 0), (pad_lo, 2 * GRID_W - pad_lo - n_dc)))
    flat = jnp.tile(ext, (1, 1, GRID_W))[:, :, :GRID_W * (2 * GRID_W - 1)]
    toe = flat.reshape(NAT_HEADS, n_dr, GRID_W, 2 * GRID_W - 1)[:, :, :, GRID_W - 1:]
    toe = jnp.where(inside[None, None], toe * LOG2E, NEG)
    tab = jnp.stack([toe[:, WIN_ROWS - 1 - cls:2 * WIN_ROWS - 1 - cls] for cls in range(WIN_ROWS)], axis=1)
    tab = tab.transpose(0, 1, 3, 2, 4).reshape(NAT_HEADS // 2, 2, WIN_ROWS, GRID_W, N_LOC)
    return tab.transpose(0, 2, 1, 3, 4).reshape(NAT_HEADS // 2, WIN_ROWS, 2 * GRID_W, N_LOC)


def _nat_attention(z, bias):
    qb = Z_QN * TN // LANES
    kb = Z_KN * TN // LANES
    vb = Z_VN * TN // LANES
    return pl.pallas_call(
        _nat_kernel,
        grid=(B, NAT_HEADS // 2),
        in_specs=[
            pl.BlockSpec((1, T, LANES), lambda b, p: (b, 0, qb + p)),
            pl.BlockSpec((1, T, LANES), lambda b, p: (b, 0, kb + p)),
            pl.BlockSpec((1, T, LANES), lambda b, p: (b, 0, vb + p)),
            pl.BlockSpec((1, WIN_ROWS, 2 * GRID_W, N_LOC), lambda b, p: (p, 0, 0, 0)),
        ],
        out_specs=pl.BlockSpec((1, T, LANES), lambda b, p: (b, 0, p)),
        out_shape=jax.ShapeDtypeStruct((B, T, BW), BF16),
        scratch_shapes=[
            pltpu.VMEM((LANES, T), BF16),
            pltpu.VMEM((T, 2 * LANES), BF16),
            pltpu.VMEM((2 * L, LANES), BF16),
            pltpu.VMEM((2 * C, LANES), BF16),
            pltpu.VMEM((2 * L, C), F32),
            pltpu.VMEM((2 * L, C), BF16),
            pltpu.VMEM((2 * L, 2 * LANES), F32),
        ],
        compiler_params=_cparams(2),
        name="nat_attention",
    )(z, z, z, bias)


def _pool_kernel(u_ref, w_ref, sc_ref, o_ref):
    for lo, n in ((0, L), (L, C)):
        t = lax.broadcasted_iota(jnp.int32, (n, LANES), 0)
        for gi, win in enumerate(POOL_WINDOWS):
            half = win // 2
            cols = slice(gi * LANES, (gi + 1) * LANES)
            u = u_ref[0, lo:lo + n, cols].astype(F32)

            def up(a, k):
                return jnp.where(t < n - k, pltpu.roll(a, n - k, 0), 0.0)

            def down(a, k):
                return jnp.where(t >= k, pltpu.roll(a, k, 0), 0.0)

            fwd = u
            bwd = down(u, 1)
            span = 1
            while span < half:
                fwd = fwd + up(fwd, span)
                bwd = bwd + down(bwd, span)
                span *= 2
            cnt = (jnp.minimum(t + half, n) - jnp.maximum(t - half, 0)).astype(F32)
            p = (fwd + bwd) / cnt - u
            y = _dot(p.astype(BF16), w_ref[gi]) * sc_ref[:, cols]
            o_ref[0, lo:lo + n, cols] = y.astype(BF16)


def _pool(z, pool_w, pool_scale):
    return pl.pallas_call(
        _pool_kernel,
        grid=(B,),
        in_specs=[
            pl.BlockSpec((1, T, BW), lambda b: (b, 0, Z_PU)),
            pl.BlockSpec((len(POOL_WINDOWS), LANES, LANES), lambda b: (0, 0, 0)),
            pl.BlockSpec((1, BW), lambda b: (0, 0)),
        ],
        out_specs=pl.BlockSpec((1, T, BW), lambda b: (b, 0, 0)),
        out_shape=jax.ShapeDtypeStruct((B, T, BW), BF16),
        compiler_params=_cparams(1),
        name="pool",
    )(z, pool_w, pool_scale)


def _mod_row(mod_ref, idx, is_ctx):
    lat = mod_ref[0, 0, idx:idx + 1, :]
    if is_ctx is None:
        return lat
    return jnp.where(is_ctx, mod_ref[0, 1, idx:idx + 1, :], lat)


def _merge_kernel(oa_ref, ob_ref, oc_ref, g_ref, x_ref, mod_ref, wb_ref, wo_ref, o_ref, *, tm, has_ctx):
    i = pl.program_id(1)
    nch = tm // RC
    for r in range(nch):
        rows = slice(r * RC, (r + 1) * RC)
        is_ctx = (i == T // tm - 1) if (has_ctx and r == nch - 1) else None
        merged = None
        for br, ref in enumerate((oa_ref, ob_ref, oc_ref)):
            gate = g_ref[0, rows, br * D:(br + 1) * D].astype(F32)
            term = gate * _dot(ref[0, rows, :], wb_ref[br])
            merged = term if merged is None else merged + term
        y = _dot(merged.astype(BF16), wo_ref[...])
        o_ref[0, rows, :] = x_ref[0, rows, :] + _mod_row(mod_ref, 2, is_ctx) * y


def _merge(oa, ob, oc, z, xs, modb, wb, wo, last):
    tm = TM_LAST if last else TM
    n_rows = L if last else T
    row_spec = lambda w: pl.BlockSpec((1, tm, w), lambda b, i: (b, i, 0))
    return pl.pallas_call(
        functools.partial(_merge_kernel, tm=tm, has_ctx=not last),
        grid=(B, n_rows // tm),
        in_specs=[
            row_spec(BW), row_spec(BW), row_spec(BW),
            pl.BlockSpec((1, tm, 3 * D), lambda b, i: (b, i, Z_GATE)),
            row_spec(D),
            pl.BlockSpec((1, 2, 6, D), lambda b, i: (b, 0, 0, 0)),
            pl.BlockSpec((3, BW, D), lambda b, i: (0, 0, 0)),
            pl.BlockSpec((D, D), lambda b, i: (0, 0)),
        ],
        out_specs=row_spec(D),
        out_shape=jax.ShapeDtypeStruct((B, n_rows, D), F32),
        compiler_params=_cparams(2),
        name="merge",
    )(oa, ob, oc, z, xs, modb, wb, wo)


def _mlp_kernel(x_ref, mod_ref, g_ref, w1_ref, w2_ref, gf_ref, o_ref, *, tm, has_ctx, final_norm):
    i = pl.program_id(1)
    nch = tm // RC
    g = g_ref[...]
    for r in range(nch):
        rows = slice(r * RC, (r + 1) * RC)
        is_ctx = (i == T // tm - 1) if (has_ctx and r == nch - 1) else None
        x = x_ref[0, rows, :]
        ms = jnp.mean(x * x, axis=-1, keepdims=True)
        y = x * lax.rsqrt(ms + EPS) * g
        h = (y * (1.0 + _mod_row(mod_ref, 4, is_ctx)) + _mod_row(mod_ref, 3, is_ctx)).astype(BF16)
        acc = None
        for f in range(D_FF // FC):
            cols = slice(f * FC, (f + 1) * FC)
            a = jnp.maximum(_dot(h, w1_ref[:, cols]), 0.0)
            part = _dot((a * a).astype(BF16), w2_ref[cols, :])
            acc = part if acc is None else acc + part
        out = x + _mod_row(mod_ref, 5, is_ctx) * acc
        if final_norm:
            ms = jnp.mean(out * out, axis=-1, keepdims=True)
            out = out * lax.rsqrt(ms + EPS) * gf_ref[...]
        o_ref[0, rows, :] = out


def _mlp(xs, modb, g, w1, w2, g_final, last):
    tm = TM_LAST if last else TM
    n_rows = xs.shape[1]
    row_spec = pl.BlockSpec((1, tm, D), lambda b, i: (b, i, 0))
    const = lambda shape: pl.BlockSpec(shape, lambda b, i: (0,) * len(shape), pipeline_mode=pl.Buffered(1))
    return pl.pallas_call(
        functools.partial(_mlp_kernel, tm=tm, has_ctx=not last, final_norm=last),
        grid=(B, n_rows // tm),
        in_specs=[
            row_spec,
            pl.BlockSpec((1, 2, 6, D), lambda b, i: (b, 0, 0, 0)),
            pl.BlockSpec((1, D), lambda b, i: (0, 0)),
            const((D, D_FF)),
            const((D_FF, D)),
            pl.BlockSpec((1, D), lambda b, i: (0, 0)),
        ],
        out_specs=row_spec,
        out_shape=jax.ShapeDtypeStruct((B, n_rows, D), F32),
        compiler_params=_cparams(2),
        name="mlp",
    )(xs, modb, g, w1, w2, g_final)


def kernel(x, c, ctx, c_ctx, w_ada, b_ada, g_mix, g_mlp, w_in, lambda_q1, lambda_k1, lambda_q2, lambda_k2,
           g_subln, pool_w, pool_scale, nat_rpb, w_branch, w_out, w_mlp1, w_mlp2, g_final):
    xs = jnp.concatenate([x, ctx], axis=1)
    pad = jnp.zeros((16 - B - 1, D), F32)
    cond = jnp.concatenate([c, c_ctx[None, :], pad], axis=0)
    mod = _adaln(cond, w_ada, b_ada)
    cos_t, sin_t = _rope_tables()

    for l in range(DEPTH):
        last = l == DEPTH - 1
        lam_init = 0.8 - 0.6 * math.exp(-0.3 * l)
        lat = mod[l, :B].reshape(B, 1, 6, D)
        cx = jnp.broadcast_to(mod[l, B].reshape(1, 1, 6, D), (B, 1, 6, D))
        modb = jnp.concatenate([lat, cx], axis=1)

        z = _inproj(xs, modb, g_mix[l][None, :], w_in, l, cos_t, sin_t)
        lparams = jnp.stack([lambda_q1[l], lambda_k1[l], lambda_q2[l], lambda_k2[l]]).astype(F32)
        gs = g_subln[l].astype(F32)[None, :]
        oa = _diff_attention(z, jnp.full((1,), lam_init, F32), lparams, gs)
        ob = _pool(z, pool_w[l].astype(BF16), pool_scale[l][None, :])
        oc = _nat_attention(z, _nat_bias_table(nat_rpb[l]))
        xs = _merge(oa, ob, oc, z, xs, modb, w_branch[l].astype(BF16), w_out[l].astype(BF16), last)
        xs = _mlp(xs, modb, g_mlp[l][None, :], w_mlp1[l].astype(BF16), w_mlp2[l].astype(BF16),
                  g_final[None, :], last)
    return xs
```

```python
import functools
import math

import numpy as np
import jax
import jax.numpy as jnp
from jax import lax
from jax.experimental import pallas as pl
from jax.experimental.pallas import tpu as pltpu

F32 = jnp.float32
BF16 = jnp.bfloat16

D = 1024
B = 8
L = 2048
C = 256
T = L + C
DEPTH = 4
GRID_W = 64
ROWS = L // GRID_W
DH = 64
DIFF_HEADS = 4
NAT_HEADS = 8
BW = 512
WIN_ROWS = 8
WIN_COLS = 16
N_LOC = WIN_ROWS * GRID_W
POOL_WINDOWS = (2, 4, 8, 16)
D_FF = 4 * D
IN_COLS = 7 * BW + 3 * D
ROPE_BASE = 10000.0
EPS = 1e-6
NEG = -1e30
LOG2E = math.log2(math.e)
Q_SCALE = DH ** -0.5 * LOG2E

LANES = 128
VMEM_LIMIT = 56 * 1024 * 1024

RC = 256
TN = 512
NJ = IN_COLS // TN
DIFF_RB = 512
DIFF_SLOTS = 2
KC = 768
NAT_RB = 512
NAT_PB = 256
NAT_BIAS_LANES = 1024
TM = 768
TM_LAST = 1024
FC = 1024

Z_GATE, Z_QA, Z_KA, Z_VA, Z_PU, Z_QN, Z_KN, Z_VN = 0, 6, 7, 8, 9, 10, 11, 12


def _cparams(n_axes):
    return pltpu.CompilerParams(dimension_semantics=("arbitrary",) * n_axes,
                                vmem_limit_bytes=VMEM_LIMIT)


def _dot(a, b):
    return jnp.dot(a, b, preferred_element_type=F32)


ADA_TN = 1536


def _adaln_kernel(c_ref, w_ref, b_ref, o_ref):
    cond = c_ref[...]
    s = (cond * jax.nn.sigmoid(cond)).astype(BF16)
    o_ref[0] = _dot(s, w_ref[0].astype(BF16)) + b_ref[0]


def _adaln(cond, w_ada, b_ada):
    n = cond.shape[0]
    return pl.pallas_call(
        _adaln_kernel,
        grid=(DEPTH, 6 * D // ADA_TN),
        in_specs=[
            pl.BlockSpec((n, D), lambda l, j: (0, 0)),
            pl.BlockSpec((1, D, ADA_TN), lambda l, j: (l, 0, j)),
            pl.BlockSpec((1, 1, ADA_TN), lambda l, j: (l, 0, j)),
        ],
        out_specs=pl.BlockSpec((1, n, ADA_TN), lambda l, j: (l, 0, j)),
        out_shape=jax.ShapeDtypeStruct((DEPTH, n, 6 * D), F32),
        compiler_params=_cparams(2),
        name="adaln",
    )(cond, w_ada, b_ada.reshape(DEPTH, 1, 6 * D))


def _inproj_kernel(x_ref, mod_ref, g_ref, w_ref, cos_ref, sin_ref, o_ref, h_s, w_s):
    j = pl.program_id(1)
    w_s[...] = w_ref[0].astype(BF16)

    @pl.when(j == 0)
    def _():
        for r0 in range(0, T, RC):
            seg = 1 if r0 >= L else 0
            xr = x_ref[0, r0:r0 + RC, :]
            ms = jnp.mean(xr * xr, axis=-1, keepdims=True)
            y = xr * lax.rsqrt(ms + EPS) * g_ref[...]
            h_s[r0:r0 + RC, :] = (y * (1.0 + mod_ref[0, seg, 1:2, :]) + mod_ref[0, seg, 0:1, :]).astype(BF16)

    def run(epilogue):
        for r0 in range(0, T, RC):
            rows = slice(r0, r0 + RC)
            acc = _dot(h_s[rows, :], w_s[...])
            epilogue(acc, rows)

    def plain(acc, rows):
        o_ref[0, rows, :] = acc.astype(BF16)

    def scaled(acc, rows):
        o_ref[0, rows, :] = (acc * Q_SCALE).astype(BF16)

    def gate(acc, rows):
        o_ref[0, rows, :] = jax.nn.sigmoid(acc).astype(BF16)

    def rope(pre):
        def epilogue(acc, rows):
            cs = cos_ref[rows, :]
            sn = sin_ref[rows, :]
            upper = (lax.broadcasted_iota(jnp.int32, (RC, LANES), 1) & 16) != 0
            for cc in range(TN // LANES):
                u = acc[:, cc * LANES:(cc + 1) * LANES]
                if pre != 1.0:
                    u = u * pre
                partner = jnp.where(upper, pltpu.roll(u, 16, 1), pltpu.roll(u, LANES - 16, 1))
                o_ref[0, rows, cc * LANES:(cc + 1) * LANES] = (u * cs + partner * sn).astype(BF16)
        return epilogue

    pl.when(j == 0)(lambda: run(rope(Q_SCALE)))
    pl.when(j == 1)(lambda: run(rope(1.0)))
    pl.when(j == 4)(lambda: run(scaled))
    pl.when(j >= 7)(lambda: run(gate))
    pl.when((j == 2) | (j == 3) | (j == 5) | (j == 6))(lambda: run(plain))


def _inproj(xs, modb, g, w_in, layer, cos_t, sin_t):
    return pl.pallas_call(
        _inproj_kernel,
        grid=(B, NJ),
        in_specs=[
            pl.BlockSpec((1, T, D), lambda b, j: (b, 0, 0)),
            pl.BlockSpec((1, 2, 6, D), lambda b, j: (b, 0, 0, 0)),
            pl.BlockSpec((1, D), lambda b, j: (0, 0)),
            pl.BlockSpec((1, D, TN), lambda b, j: (layer, 0, j)),
            pl.BlockSpec((T, LANES), lambda b, j: (0, 0)),
            pl.BlockSpec((T, LANES), lambda b, j: (0, 0)),
        ],
        out_specs=pl.BlockSpec((1, T, TN), lambda b, j: (b, 0, (j + 6) % NJ)),
        out_shape=jax.ShapeDtypeStruct((B, T, IN_COLS), BF16),
        scratch_shapes=[pltpu.VMEM((T, D), BF16), pltpu.VMEM((D, TN), BF16)],
        compiler_params=_cparams(2),
        name="inproj",
    )(xs, modb, g, w_in, cos_t, sin_t)


def _rope_tables():
    nf = DH // 4
    pos = jnp.arange(L)
    row = (pos // GRID_W).astype(F32)
    col = (pos % GRID_W).astype(F32)
    inv = ROPE_BASE ** (-jnp.arange(nf, dtype=F32) / nf)

    def part(p):
        ang = p[:, None] * inv
        cs = jnp.cos(ang).astype(F32)
        sn = jnp.sin(ang).astype(F32)
        return jnp.concatenate([cs, cs], axis=1), jnp.concatenate([-sn, sn], axis=1)

    cr, sr = part(row)
    cc, sc = part(col)
    cos_h = jnp.concatenate([cr, cc], axis=1)
    sin_h = jnp.concatenate([sr, sc], axis=1)
    reps = LANES // DH
    cos_t = jnp.concatenate([jnp.tile(cos_h, (1, reps)), jnp.ones((C, LANES), F32)], axis=0)
    sin_t = jnp.concatenate([jnp.tile(sin_h, (1, reps)), jnp.zeros((C, LANES), F32)], axis=0)
    return cos_t, sin_t


def _split_heads(k_ref, lo_s, hi_s):
    kt = k_ref[0].astype(F32).T
    row = lax.broadcasted_iota(jnp.int32, (LANES, T), 0)
    lo_s[...] = jnp.where(row < DH, kt, 0.0).astype(BF16)
    hi_s[...] = jnp.where(row >= DH, kt, 0.0).astype(BF16)


def _extend_values(v_ref, vx_s):
    vx_s[:, :LANES] = v_ref[0]
    vx_s[:, LANES:] = jnp.ones((T, LANES), BF16)


def _diff_kernel(lam0_ref, q_ref, k_ref, v_ref, lp_ref, gs_ref, o_ref, k0_s, k1_s, vx_s, s0_s, s1_s):
    _split_heads(k_ref, k0_s, k1_s)
    _extend_values(v_ref, vx_s)
    rb = DIFF_RB

    lam_init = lam0_ref[0]
    lp = lp_ref[...]
    lam = (jnp.exp(jnp.sum(lp[0:1] * lp[1:2], axis=-1, keepdims=True))
           - jnp.exp(jnp.sum(lp[2:3] * lp[3:4], axis=-1, keepdims=True)) + lam_init)
    maps = ((k0_s, s0_s), (k1_s, s1_s))

    def attend(r_lo, r_hi, k_lo, k_hi, kcs):
        rows = slice(r_lo, r_hi)
        slot = (r_lo // rb) % DIFF_SLOTS
        srows = slice(slot * rb, slot * rb + (r_hi - r_lo))
        q = q_ref[0, rows, :]
        chunks = [(c0, slice(c0 - k_lo, c0 - k_lo + kcs)) for c0 in range(k_lo, k_hi, kcs)]
        m = [None, None]
        for c0, cols in chunks:
            for mi, (ks, ss) in enumerate(maps):
                s = _dot(q, ks[:, c0:c0 + kcs])
                ss[srows, cols] = s
                mc = jnp.max(s, axis=-1, keepdims=True)
                m[mi] = mc if m[mi] is None else jnp.maximum(m[mi], mc)
        acc = [None, None]
        for c0, cols in chunks:
            for mi, (ks, ss) in enumerate(maps):
                e = jnp.exp2(ss[srows, cols] - m[mi]).astype(BF16)
                part = _dot(e, vx_s[c0:c0 + kcs, :])
                acc[mi] = part if acc[mi] is None else acc[mi] + part
        heads = [a[:, :LANES] / a[:, LANES:] for a in acc]
        o = heads[0] - lam * heads[1]
        ms = jnp.mean(o * o, axis=-1, keepdims=True)
        y = o * lax.rsqrt(ms + EPS) * gs_ref[...]
        o_ref[0, rows, :] = (y * (1.0 - lam_init)).astype(BF16)

    for r0 in range(0, L, rb):
        attend(r0, r0 + rb, 0, T, KC)
    attend(L, T, L, T, C)


def _diff_attention(z, lam0, lparams, g_subln):
    qb = Z_QA * TN // LANES
    kb = Z_KA * TN // LANES
    vb = Z_VA * TN // LANES
    return pl.pallas_call(
        _diff_kernel,
        grid=(B, DIFF_HEADS),
        in_specs=[
            pl.BlockSpec(memory_space=pltpu.SMEM),
            pl.BlockSpec((1, T, LANES), lambda b, h: (b, 0, qb + h)),
            pl.BlockSpec((1, T, LANES), lambda b, h: (b, 0, kb + h)),
            pl.BlockSpec((1, T, LANES), lambda b, h: (b, 0, vb + h)),
            pl.BlockSpec((4, DH), lambda b, h: (0, 0)),
            pl.BlockSpec((1, LANES), lambda b, h: (0, 0)),
        ],
        out_specs=pl.BlockSpec((1, T, LANES), lambda b, h: (b, 0, h)),
        out_shape=jax.ShapeDtypeStruct((B, T, BW), BF16),
        scratch_shapes=[
            pltpu.VMEM((LANES, T), BF16),
            pltpu.VMEM((LANES, T), BF16),
            pltpu.VMEM((T, 2 * LANES), BF16),
            pltpu.VMEM((DIFF_SLOTS * DIFF_RB, T), F32),
            pltpu.VMEM((DIFF_SLOTS * DIFF_RB, T), F32),
        ],
        compiler_params=_cparams(2),
        name="diff_attention",
    )(lam0, z, z, z, lparams, g_subln)


def _nat_kernel(q_ref, k_ref, v_ref, bias_ref, o_ref, kt_s, vx_s, qq_s, qc_s, sctx_s, ectx_s, acc_s,
                sloc_s, eloc_s):
    kt_s[...] = k_ref[0].astype(F32).T.astype(BF16)
    _extend_values(v_ref, vx_s)
    ctx = slice(L, T)
    blk = 2 * GRID_W

    def stack_heads(dst, n_blocks, src_lo, rows_per_block):
        for i in range(n_blocks):
            qf = q_ref[0, src_lo + i * rows_per_block:src_lo + (i + 1) * rows_per_block, :].astype(F32)
            low = lax.broadcasted_iota(jnp.int32, qf.shape, 1) < DH
            dst[2 * i * rows_per_block:(2 * i + 1) * rows_per_block, :] = jnp.where(low, qf, 0.0).astype(BF16)
            dst[(2 * i + 1) * rows_per_block:(2 * i + 2) * rows_per_block, :] = jnp.where(low, 0.0, qf).astype(BF16)

    def unstack(res, n):
        low = lax.broadcasted_iota(jnp.int32, (n, LANES), 1) < DH
        return jnp.where(low, res[:n], res[n:]).astype(BF16)

    stack_heads(qq_s, ROWS, 0, GRID_W)
    stack_heads(qc_s, 1, L, C)

    s = _dot(qc_s[...], kt_s[:, ctx])
    e = jnp.exp2(s - jnp.max(s, axis=-1, keepdims=True)).astype(BF16)
    acc = _dot(e, vx_s[ctx, :])
    o_ref[0, ctx, :] = unstack(acc[:, :LANES] / acc[:, LANES:], C)

    for r0 in range(0, 2 * L, NAT_RB):
        sctx_s[r0:r0 + NAT_RB, :] = _dot(qq_s[r0:r0 + NAT_RB, :], kt_s[:, ctx])

    def row_window(r):
        rs = min(max(r - WIN_ROWS // 2, 0), ROWS - WIN_ROWS)
        return slice(r * blk, (r + 1) * blk), slice(rs * GRID_W, rs * GRID_W + N_LOC)

    def local_scores(r):
        rs = min(max(r - WIN_ROWS // 2, 0), ROWS - WIN_ROWS)
        rows, keys = row_window(r)
        start = (WIN_ROWS - 1 - (r - rs)) * GRID_W
        copy, start = (0, start) if start % LANES == 0 else (1, start - GRID_W)
        bias = jnp.concatenate([bias_ref[0, hh, copy, :, start:start + N_LOC] for hh in range(2)], axis=0)
        return rows, keys, _dot(qq_s[rows, :], kt_s[:, keys]) + bias

    for r in range(ROWS):
        rows, _, s_loc = local_scores(r)
        sloc_s[rows, :] = s_loc
    for r0 in range(0, 2 * L, NAT_PB):
        rows = slice(r0, r0 + NAT_PB)
        s_loc = sloc_s[rows, :]
        s_ctx = sctx_s[rows, :]
        m = jnp.maximum(jnp.max(s_loc, axis=-1, keepdims=True), jnp.max(s_ctx, axis=-1, keepdims=True))
        ectx_s[rows, :] = jnp.exp2(s_ctx - m).astype(BF16)
        eloc_s[rows, :] = jnp.exp2(s_loc - m).astype(BF16)
    for r in range(ROWS):
        rows, keys = row_window(r)
        acc_s[rows, :] = _dot(eloc_s[rows, :], vx_s[keys, :])

    for r0 in range(0, 2 * L, NAT_RB):
        rows = slice(r0, r0 + NAT_RB)
        acc = acc_s[rows, :] + _dot(ectx_s[rows, :], vx_s[ctx, :])
        res = acc[:, :LANES] / acc[:, LANES:]
        for i in range(NAT_RB // blk):
            r = r0 // blk + i
            o_ref[0, r * GRID_W:(r + 1) * GRID_W, :] = unstack(res[i * blk:(i + 1) * blk], GRID_W)


def _nat_bias_table(rpb):
    n_dr, n_dc = 2 * WIN_ROWS - 1, 2 * WIN_COLS - 1
    rpb = rpb.reshape(DEPTH * NAT_HEADS, n_dr, n_dc)
    w = np.arange(GRID_W)
    cc = np.arange(GRID_W)
    start = np.clip(w - WIN_COLS // 2, 0, GRID_W - WIN_COLS)
    inside = (cc[None, :] >= start[:, None]) & (cc[None, :] < start[:, None] + WIN_COLS)
    pad_lo = GRID_W - 1 - (WIN_COLS - 1)
    ext = jnp.pad(rpb.astype(F32), ((0, 0), (0, 0), (pad_lo, 2 * GRID_W - pad_lo - n_dc)))
    flat = jnp.tile(ext, (1, 1, GRID_W))[:, :, :GRID_W * (2 * GRID_W - 1)]
    toe = flat.reshape(-1, n_dr, GRID_W, 2 * GRID_W - 1)[:, :, :, GRID_W - 1:]
    toe = jnp.where(inside[None, None], toe * LOG2E, NEG).transpose(0, 2, 1, 3)
    flat = toe.reshape(-1, GRID_W, n_dr * GRID_W)
    copy0 = jnp.pad(flat, ((0, 0), (0, 0), (0, NAT_BIAS_LANES - n_dr * GRID_W)))
    copy1 = jnp.pad(flat[:, :, GRID_W:], ((0, 0), (0, 0), (0, NAT_BIAS_LANES - (n_dr - 1) * GRID_W)))
    return jnp.stack([copy0, copy1], axis=1).reshape(DEPTH, NAT_HEADS, 2, GRID_W, NAT_BIAS_LANES)


def _nat_attention(z, bias, layer):
    qb = Z_QN * TN // LANES
    kb = Z_KN * TN // LANES
    vb = Z_VN * TN // LANES
    return pl.pallas_call(
        _nat_kernel,
        grid=(B, NAT_HEADS // 2),
        in_specs=[
            pl.BlockSpec((1, T, LANES), lambda b, p: (b, 0, qb + p)),
            pl.BlockSpec((1, T, LANES), lambda b, p: (b, 0, kb + p)),
            pl.BlockSpec((1, T, LANES), lambda b, p: (b, 0, vb + p)),
            pl.BlockSpec((1, 2, 2, GRID_W, NAT_BIAS_LANES), lambda b, p: (layer, p, 0, 0, 0)),
        ],
        out_specs=pl.BlockSpec((1, T, LANES), lambda b, p: (b, 0, p)),
        out_shape=jax.ShapeDtypeStruct((B, T, BW), BF16),
        scratch_shapes=[
            pltpu.VMEM((LANES, T), BF16),
            pltpu.VMEM((T, 2 * LANES), BF16),
            pltpu.VMEM((2 * L, LANES), BF16),
            pltpu.VMEM((2 * C, LANES), BF16),
            pltpu.VMEM((2 * L, C), F32),
            pltpu.VMEM((2 * L, C), BF16),
            pltpu.VMEM((2 * L, 2 * LANES), F32),
            pltpu.VMEM((2 * L, N_LOC), F32),
            pltpu.VMEM((2 * L, N_LOC), BF16),
        ],
        compiler_params=_cparams(2),
        name="nat_attention",
    )(z, z, z, bias)


def _pool_kernel(u_ref, w_ref, sc_ref, o_ref):
    for lo, n in ((0, L), (L, C)):
        t = lax.broadcasted_iota(jnp.int32, (n, LANES), 0)
        for gi, win in enumerate(POOL_WINDOWS):
            half = win // 2
            cols = slice(gi * LANES, (gi + 1) * LANES)
            u = u_ref[0, lo:lo + n, cols].astype(F32)

            def up(a, k):
                return jnp.where(t < n - k, pltpu.roll(a, n - k, 0), 0.0)

            def down(a, k):
                return jnp.where(t >= k, pltpu.roll(a, k, 0), 0.0)

            fwd = u
            bwd = down(u, 1)
            span = 1
            while span < half:
                fwd = fwd + up(fwd, span)
                bwd = bwd + down(bwd, span)
                span *= 2
            cnt = (jnp.minimum(t + half, n) - jnp.maximum(t - half, 0)).astype(F32)
            p = (fwd + bwd) / cnt - u
            y = _dot(p.astype(BF16), w_ref[gi]) * sc_ref[:, cols]
            o_ref[0, lo:lo + n, cols] = y.astype(BF16)


def _pool(z, pool_w, pool_scale):
    return pl.pallas_call(
        _pool_kernel,
        grid=(B,),
        in_specs=[
            pl.BlockSpec((1, T, BW), lambda b: (b, 0, Z_PU)),
            pl.BlockSpec((len(POOL_WINDOWS), LANES, LANES), lambda b: (0, 0, 0)),
            pl.BlockSpec((1, BW), lambda b: (0, 0)),
        ],
        out_specs=pl.BlockSpec((1, T, BW), lambda b: (b, 0, 0)),
        out_shape=jax.ShapeDtypeStruct((B, T, BW), BF16),
        compiler_params=_cparams(1),
        name="pool",
    )(z, pool_w, pool_scale)


def _mod_row(mod_ref, idx, is_ctx):
    lat = mod_ref[0, 0, idx:idx + 1, :]
    if is_ctx is None:
        return lat
    return jnp.where(is_ctx, mod_ref[0, 1, idx:idx + 1, :], lat)


def _ctx_flag(i, tm, has_ctx, row0):
    if not has_ctx or row0 != tm - C:
        return None
    return i == T // tm - 1


def _merge_kernel(oa_ref, ob_ref, oc_ref, g_ref, x_ref, mod_ref, wb_ref, wo_ref, o_ref, *, tm, has_ctx):
    i = pl.program_id(1)
    merged = None
    for br, ref in enumerate((oa_ref, ob_ref, oc_ref)):
        gate = g_ref[0, :, br * D:(br + 1) * D].astype(F32)
        term = gate * _dot(ref[0], wb_ref[0, br])
        merged = term if merged is None else merged + term
    y = _dot(merged.astype(BF16), wo_ref[0])
    for r0 in range(0, tm, RC):
        rows = slice(r0, r0 + RC)
        gate1 = _mod_row(mod_ref, 2, _ctx_flag(i, tm, has_ctx, r0))
        o_ref[0, rows, :] = x_ref[0, rows, :] + gate1 * y[rows]


def _merge(oa, ob, oc, z, xs, modb, wb, wo, layer, last):
    tm = TM_LAST if last else TM
    n_rows = L if last else T
    row_spec = lambda w: pl.BlockSpec((1, tm, w), lambda b, i: (b, i, 0))
    return pl.pallas_call(
        functools.partial(_merge_kernel, tm=tm, has_ctx=not last),
        grid=(B, n_rows // tm),
        in_specs=[
            row_spec(BW), row_spec(BW), row_spec(BW),
            pl.BlockSpec((1, tm, 3 * D), lambda b, i: (b, i, Z_GATE)),
            row_spec(D),
            pl.BlockSpec((1, 2, 6, D), lambda b, i: (b, 0, 0, 0)),
            pl.BlockSpec((1, 3, BW, D), lambda b, i: (layer, 0, 0, 0)),
            pl.BlockSpec((1, D, D), lambda b, i: (layer, 0, 0)),
        ],
        out_specs=row_spec(D),
        out_shape=jax.ShapeDtypeStruct((B, n_rows, D), F32),
        compiler_params=_cparams(2),
        name="merge",
    )(oa, ob, oc, z, xs, modb, wb, wo)


def _mlp_kernel(x_ref, mod_ref, g_ref, w1_ref, w2_ref, gf_ref, o_ref, h_s, *, tm, has_ctx, final_norm):
    i = pl.program_id(1)
    g = g_ref[...]
    for r0 in range(0, tm, RC):
        rows = slice(r0, r0 + RC)
        is_ctx = _ctx_flag(i, tm, has_ctx, r0)
        x = x_ref[0, rows, :]
        ms = jnp.mean(x * x, axis=-1, keepdims=True)
        y = x * lax.rsqrt(ms + EPS) * g
        h_s[rows, :] = (y * (1.0 + _mod_row(mod_ref, 4, is_ctx)) + _mod_row(mod_ref, 3, is_ctx)).astype(BF16)
    h = h_s[...]
    acc = None
    for f in range(D_FF // FC):
        cols = slice(f * FC, (f + 1) * FC)
        a = jnp.maximum(_dot(h, w1_ref[0, :, cols]), 0.0)
        part = _dot((a * a).astype(BF16), w2_ref[0, cols, :])
        acc = part if acc is None else acc + part
    for r0 in range(0, tm, RC):
        rows = slice(r0, r0 + RC)
        gate2 = _mod_row(mod_ref, 5, _ctx_flag(i, tm, has_ctx, r0))
        out = x_ref[0, rows, :] + gate2 * acc[rows]
        if final_norm:
            ms = jnp.mean(out * out, axis=-1, keepdims=True)
            out = out * lax.rsqrt(ms + EPS) * gf_ref[...]
        o_ref[0, rows, :] = out


def _mlp(xs, modb, g, w1, w2, g_final, layer, last):
    tm = TM_LAST if last else TM
    n_rows = xs.shape[1]
    row_spec = pl.BlockSpec((1, tm, D), lambda b, i: (b, i, 0))
    const = lambda shape: pl.BlockSpec((1,) + shape, lambda b, i: (layer, 0, 0), pipeline_mode=pl.Buffered(1))
    return pl.pallas_call(
        functools.partial(_mlp_kernel, tm=tm, has_ctx=not last, final_norm=last),
        grid=(B, n_rows // tm),
        in_specs=[
            row_spec,
            pl.BlockSpec((1, 2, 6, D), lambda b, i: (b, 0, 0, 0)),
            pl.BlockSpec((1, D), lambda b, i: (0, 0)),
            const((D, D_FF)),
            const((D_FF, D)),
            pl.BlockSpec((1, D), lambda b, i: (0, 0)),
        ],
        out_specs=row_spec,
        out_shape=jax.ShapeDtypeStruct((B, n_rows, D), F32),
        scratch_shapes=[pltpu.VMEM((tm, D), BF16)],
        compiler_params=_cparams(2),
        name="mlp",
    )(xs, modb, g, w1, w2, g_final)


def kernel(x, c, ctx, c_ctx, w_ada, b_ada, g_mix, g_mlp, w_in, lambda_q1, lambda_k1, lambda_q2, lambda_k2,
           g_subln, pool_w, pool_scale, nat_rpb, w_branch, w_out, w_mlp1, w_mlp2, g_final):
    xs = jnp.concatenate([x, ctx], axis=1)
    pad = jnp.zeros((16 - B - 1, D), F32)
    cond = jnp.concatenate([c, c_ctx[None, :], pad], axis=0)
    mod = _adaln(cond, w_ada, b_ada)
    cos_t, sin_t = _rope_tables()
    bias = _nat_bias_table(nat_rpb)
    wb, wo = w_branch.astype(BF16), w_out.astype(BF16)
    w1, w2 = w_mlp1.astype(BF16), w_mlp2.astype(BF16)

    for l in range(DEPTH):
        last = l == DEPTH - 1
        lam_init = 0.8 - 0.6 * math.exp(-0.3 * l)
        lat = mod[l, :B].reshape(B, 1, 6, D)
        cx = jnp.broadcast_to(mod[l, B].reshape(1, 1, 6, D), (B, 1, 6, D))
        modb = jnp.concatenate([lat, cx], axis=1)

        z = _inproj(xs, modb, g_mix[l][None, :], w_in, l, cos_t, sin_t)
        lparams = jnp.stack([lambda_q1[l], lambda_k1[l], lambda_q2[l], lambda_k2[l]]).astype(F32)
        gs = g_subln[l].astype(F32)[None, :]
        oa = _diff_attention(z, jnp.full((1,), lam_init, F32), lparams, gs)
        ob = _pool(z, pool_w[l].astype(BF16), pool_scale[l][None, :])
        oc = _nat_attention(z, bias, l)
        xs = _merge(oa, ob, oc, z, xs, modb, wb, wo, l, last)
        xs = _mlp(xs, modb, g_mlp[l][None, :], w1, w2, g_final[None, :], l, last)
    return xs
```

```python
import functools
import math

import numpy as np
import jax
import jax.numpy as jnp
from jax import lax
from jax.experimental import pallas as pl
from jax.experimental.pallas import tpu as pltpu

F32 = jnp.float32
BF16 = jnp.bfloat16

D = 1024
B = 8
L = 2048
C = 256
T = L + C
DEPTH = 4
GRID_W = 64
ROWS = L // GRID_W
DH = 64
DIFF_HEADS = 4
NAT_HEADS = 8
BW = 512
WIN_ROWS = 8
WIN_COLS = 16
N_LOC = WIN_ROWS * GRID_W
POOL_WINDOWS = (2, 4, 8, 16)
D_FF = 4 * D
IN_COLS = 7 * BW + 3 * D
ROPE_BASE = 10000.0
EPS = 1e-6
NEG = -1e30
LOG2E = math.log2(math.e)
Q_SCALE = DH ** -0.5 * LOG2E

LANES = 128
VMEM_LIMIT = 56 * 1024 * 1024

RC = 256
TN = 512
NJ = IN_COLS // TN
DIFF_RB = 1024
KC = 768
NAT_RB = 512
NAT_PB = 256
NAT_BIAS_LANES = 1024
TM = 768
TM_LAST = 1024
FC = 1024

Z_GATE, Z_QA, Z_KA, Z_VA, Z_PU, Z_QN, Z_KN, Z_VN = 0, 6, 7, 8, 9, 10, 11, 12


def _cparams(n_axes):
    return pltpu.CompilerParams(dimension_semantics=("arbitrary",) * n_axes,
                                vmem_limit_bytes=VMEM_LIMIT)


def _dot(a, b):
    return jnp.dot(a, b, preferred_element_type=F32)


ADA_TN = 1536


def _adaln_kernel(c_ref, w_ref, b_ref, o_ref):
    cond = c_ref[...]
    s = (cond * jax.nn.sigmoid(cond)).astype(BF16)
    o_ref[0] = _dot(s, w_ref[0].astype(BF16)) + b_ref[0]


def _adaln(cond, w_ada, b_ada):
    n = cond.shape[0]
    return pl.pallas_call(
        _adaln_kernel,
        grid=(DEPTH, 6 * D // ADA_TN),
        in_specs=[
            pl.BlockSpec((n, D), lambda l, j: (0, 0)),
            pl.BlockSpec((1, D, ADA_TN), lambda l, j: (l, 0, j)),
            pl.BlockSpec((1, 1, ADA_TN), lambda l, j: (l, 0, j)),
        ],
        out_specs=pl.BlockSpec((1, n, ADA_TN), lambda l, j: (l, 0, j)),
        out_shape=jax.ShapeDtypeStruct((DEPTH, n, 6 * D), F32),
        compiler_params=_cparams(2),
        name="adaln",
    )(cond, w_ada, b_ada.reshape(DEPTH, 1, 6 * D))


def _inproj_kernel(x_ref, mod_ref, g_ref, w_ref, cos_ref, sin_ref, o_ref, h_s, w_s, *, ctx_all):
    j = pl.program_id(1)
    w_s[...] = w_ref[0].astype(BF16)

    @pl.when(j == 0)
    def _():
        for r0 in range(0, T, RC):
            seg = 1 if r0 >= L else 0
            xr = x_ref[0, r0:r0 + RC, :]
            ms = jnp.mean(xr * xr, axis=-1, keepdims=True)
            y = xr * lax.rsqrt(ms + EPS) * g_ref[...]
            h_s[r0:r0 + RC, :] = (y * (1.0 + mod_ref[0, seg, 1:2, :]) + mod_ref[0, seg, 0:1, :]).astype(BF16)

    def run(epilogue, with_ctx=True):
        for r0 in range(0, T if with_ctx else L, RC):
            rows = slice(r0, r0 + RC)
            acc = _dot(h_s[rows, :], w_s[...])
            epilogue(acc, rows)
        if not with_ctx:
            o_ref[0, L:T, :] = jnp.zeros((C, TN), BF16)

    def plain(acc, rows):
        o_ref[0, rows, :] = acc.astype(BF16)

    def scaled(acc, rows):
        o_ref[0, rows, :] = (acc * Q_SCALE).astype(BF16)

    def gate(acc, rows):
        o_ref[0, rows, :] = jax.nn.sigmoid(acc).astype(BF16)

    def rope(pre):
        def epilogue(acc, rows):
            cs = cos_ref[rows, :]
            sn = sin_ref[rows, :]
            upper = (lax.broadcasted_iota(jnp.int32, (RC, LANES), 1) & 16) != 0
            for cc in range(TN // LANES):
                u = acc[:, cc * LANES:(cc + 1) * LANES]
                if pre != 1.0:
                    u = u * pre
                partner = jnp.where(upper, pltpu.roll(u, 16, 1), pltpu.roll(u, LANES - 16, 1))
                o_ref[0, rows, cc * LANES:(cc + 1) * LANES] = (u * cs + partner * sn).astype(BF16)
        return epilogue

    pl.when(j == 0)(lambda: run(rope(Q_SCALE), ctx_all))
    pl.when(j == 1)(lambda: run(rope(1.0)))
    pl.when(j == 4)(lambda: run(scaled, ctx_all))
    pl.when(j >= 7)(lambda: run(gate, ctx_all))
    pl.when((j == 2) | (j == 5) | (j == 6))(lambda: run(plain))
    pl.when(j == 3)(lambda: run(plain, ctx_all))


def _inproj(xs, modb, g, w_in, layer, cos_t, sin_t, ctx_all):
    return pl.pallas_call(
        functools.partial(_inproj_kernel, ctx_all=ctx_all),
        grid=(B, NJ),
        in_specs=[
            pl.BlockSpec((1, T, D), lambda b, j: (b, 0, 0)),
            pl.BlockSpec((1, 2, 6, D), lambda b, j: (b, 0, 0, 0)),
            pl.BlockSpec((1, D), lambda b, j: (0, 0)),
            pl.BlockSpec((1, D, TN), lambda b, j: (layer, 0, j)),
            pl.BlockSpec((T, LANES), lambda b, j: (0, 0)),
            pl.BlockSpec((T, LANES), lambda b, j: (0, 0)),
        ],
        out_specs=pl.BlockSpec((1, T, TN), lambda b, j: (b, 0, (j + 6) % NJ)),
        out_shape=jax.ShapeDtypeStruct((B, T, IN_COLS), BF16),
        scratch_shapes=[pltpu.VMEM((T, D), BF16), pltpu.VMEM((D, TN), BF16)],
        compiler_params=_cparams(2),
        name="inproj",
    )(xs, modb, g, w_in, cos_t, sin_t)


def _rope_tables():
    nf = DH // 4
    pos = jnp.arange(L)
    row = (pos // GRID_W).astype(F32)
    col = (pos % GRID_W).astype(F32)
    inv = ROPE_BASE ** (-jnp.arange(nf, dtype=F32) / nf)

    def part(p):
        ang = p[:, None] * inv
        cs = jnp.cos(ang).astype(F32)
        sn = jnp.sin(ang).astype(F32)
        return jnp.concatenate([cs, cs], axis=1), jnp.concatenate([-sn, sn], axis=1)

    cr, sr = part(row)
    cc, sc = part(col)
    cos_h = jnp.concatenate([cr, cc], axis=1)
    sin_h = jnp.concatenate([sr, sc], axis=1)
    reps = LANES // DH
    cos_t = jnp.concatenate([jnp.tile(cos_h, (1, reps)), jnp.ones((C, LANES), F32)], axis=0)
    sin_t = jnp.concatenate([jnp.tile(sin_h, (1, reps)), jnp.zeros((C, LANES), F32)], axis=0)
    return cos_t, sin_t


def _split_heads(k_ref, lo_s, hi_s):
    kt = k_ref[0].astype(F32).T
    row = lax.broadcasted_iota(jnp.int32, (LANES, T), 0)
    lo_s[...] = jnp.where(row < DH, kt, 0.0).astype(BF16)
    hi_s[...] = jnp.where(row >= DH, kt, 0.0).astype(BF16)


def _extend_values(v_ref, vx_s):
    vx_s[:, :LANES] = v_ref[0]
    vx_s[:, LANES:] = jnp.ones((T, LANES), BF16)


def _diff_kernel(lam0_ref, q_ref, k_ref, v_ref, lp_ref, gs_ref, o_ref, k0_s, k1_s, vx_s, s0_s, s1_s, *, ctx_queries):
    _split_heads(k_ref, k0_s, k1_s)
    _extend_values(v_ref, vx_s)

    lam_init = lam0_ref[0]
    lp = lp_ref[...]
    lam = (jnp.exp(jnp.sum(lp[0:1] * lp[1:2], axis=-1, keepdims=True))
           - jnp.exp(jnp.sum(lp[2:3] * lp[3:4], axis=-1, keepdims=True)) + lam_init)
    maps = ((k0_s, s0_s), (k1_s, s1_s))

    def attend(r_lo, r_hi, k_lo, k_hi, kcs):
        rows = slice(r_lo, r_hi)
        srows = slice(0, r_hi - r_lo)
        q = q_ref[0, rows, :]
        chunks = [(c0, slice(c0 - k_lo, c0 - k_lo + kcs)) for c0 in range(k_lo, k_hi, kcs)]
        m = [None, None]
        for c0, cols in chunks:
            for mi, (ks, ss) in enumerate(maps):
                s = _dot(q, ks[:, c0:c0 + kcs])
                ss[srows, cols] = s
                mc = jnp.max(s, axis=-1, keepdims=True)
                m[mi] = mc if m[mi] is None else jnp.maximum(m[mi], mc)
        acc = [None, None]
        for c0, cols in chunks:
            for mi, (ks, ss) in enumerate(maps):
                e = jnp.exp2(ss[srows, cols] - m[mi]).astype(BF16)
                part = _dot(e, vx_s[c0:c0 + kcs, :])
                acc[mi] = part if acc[mi] is None else acc[mi] + part
        heads = [a[:, :LANES] / a[:, LANES:] for a in acc]
        o = heads[0] - lam * heads[1]
        ms = jnp.mean(o * o, axis=-1, keepdims=True)
        y = o * lax.rsqrt(ms + EPS) * gs_ref[...]
        o_ref[0, rows, :] = (y * (1.0 - lam_init)).astype(BF16)

    for r0 in range(0, L, DIFF_RB):
        attend(r0, r0 + DIFF_RB, 0, T, KC)
    if ctx_queries:
        attend(L, T, L, T, C)
    else:
        o_ref[0, L:T, :] = jnp.zeros((C, LANES), BF16)


def _diff_attention(z, lam0, lparams, g_subln, ctx_queries):
    qb = Z_QA * TN // LANES
    kb = Z_KA * TN // LANES
    vb = Z_VA * TN // LANES
    return pl.pallas_call(
        functools.partial(_diff_kernel, ctx_queries=ctx_queries),
        grid=(B, DIFF_HEADS),
        in_specs=[
            pl.BlockSpec(memory_space=pltpu.SMEM),
            pl.BlockSpec((1, T, LANES), lambda b, h: (b, 0, qb + h)),
            pl.BlockSpec((1, T, LANES), lambda b, h: (b, 0, kb + h)),
            pl.BlockSpec((1, T, LANES), lambda b, h: (b, 0, vb + h)),
            pl.BlockSpec((4, DH), lambda b, h: (0, 0)),
            pl.BlockSpec((1, LANES), lambda b, h: (0, 0)),
        ],
        out_specs=pl.BlockSpec((1, T, LANES), lambda b, h: (b, 0, h)),
        out_shape=jax.ShapeDtypeStruct((B, T, BW), BF16),
        scratch_shapes=[
            pltpu.VMEM((LANES, T), BF16),
            pltpu.VMEM((LANES, T), BF16),
            pltpu.VMEM((T, 2 * LANES), BF16),
            pltpu.VMEM((DIFF_RB, T), F32),
            pltpu.VMEM((DIFF_RB, T), F32),
        ],
        compiler_params=_cparams(2),
        name="diff_attention",
    )(lam0, z, z, z, lparams, g_subln)


def _nat_kernel(q_ref, k_ref, v_ref, bias_ref, o_ref, kt_s, vx_s, qq_s, qc_s, sctx_s, ectx_s, acc_s,
                sloc_s, eloc_s, *, ctx_queries):
    kt_s[...] = k_ref[0].astype(F32).T.astype(BF16)
    _extend_values(v_ref, vx_s)
    ctx = slice(L, T)
    blk = 2 * GRID_W

    def stack_heads(dst, n_blocks, src_lo, rows_per_block):
        for i in range(n_blocks):
            qf = q_ref[0, src_lo + i * rows_per_block:src_lo + (i + 1) * rows_per_block, :].astype(F32)
            low = lax.broadcasted_iota(jnp.int32, qf.shape, 1) < DH
            dst[2 * i * rows_per_block:(2 * i + 1) * rows_per_block, :] = jnp.where(low, qf, 0.0).astype(BF16)
            dst[(2 * i + 1) * rows_per_block:(2 * i + 2) * rows_per_block, :] = jnp.where(low, 0.0, qf).astype(BF16)

    def unstack(res, n):
        low = lax.broadcasted_iota(jnp.int32, (n, LANES), 1) < DH
        return jnp.where(low, res[:n], res[n:]).astype(BF16)

    stack_heads(qq_s, ROWS, 0, GRID_W)

    if ctx_queries:
        stack_heads(qc_s, 1, L, C)
        s = _dot(qc_s[...], kt_s[:, ctx])
        e = jnp.exp2(s - jnp.max(s, axis=-1, keepdims=True)).astype(BF16)
        acc = _dot(e, vx_s[ctx, :])
        o_ref[0, ctx, :] = unstack(acc[:, :LANES] / acc[:, LANES:], C)
    else:
        o_ref[0, ctx, :] = jnp.zeros((C, LANES), BF16)

    for r0 in range(0, 2 * L, NAT_RB):
        sctx_s[r0:r0 + NAT_RB, :] = _dot(qq_s[r0:r0 + NAT_RB, :], kt_s[:, ctx])

    def row_window(r):
        rs = min(max(r - WIN_ROWS // 2, 0), ROWS - WIN_ROWS)
        return slice(r * blk, (r + 1) * blk), slice(rs * GRID_W, rs * GRID_W + N_LOC)

    def local_scores(r):
        rs = min(max(r - WIN_ROWS // 2, 0), ROWS - WIN_ROWS)
        rows, keys = row_window(r)
        start = (WIN_ROWS - 1 - (r - rs)) * GRID_W
        copy, start = (0, start) if start % LANES == 0 else (1, start - GRID_W)
        bias = jnp.concatenate([bias_ref[0, hh, copy, :, start:start + N_LOC] for hh in range(2)], axis=0)
        return rows, keys, _dot(qq_s[rows, :], kt_s[:, keys]) + bias

    for r in range(ROWS):
        rows, _, s_loc = local_scores(r)
        sloc_s[rows, :] = s_loc
    for r0 in range(0, 2 * L, NAT_PB):
        rows = slice(r0, r0 + NAT_PB)
        s_loc = sloc_s[rows, :]
        s_ctx = sctx_s[rows, :]
        m = jnp.maximum(jnp.max(s_loc, axis=-1, keepdims=True), jnp.max(s_ctx, axis=-1, keepdims=True))
        ectx_s[rows, :] = jnp.exp2(s_ctx - m).astype(BF16)
        eloc_s[rows, :] = jnp.exp2(s_loc - m).astype(BF16)
    for r in range(ROWS):
        rows, keys = row_window(r)
        acc_s[rows, :] = _dot(eloc_s[rows, :], vx_s[keys, :])

    for r0 in range(0, 2 * L, NAT_RB):
        rows = slice(r0, r0 + NAT_RB)
        acc = acc_s[rows, :] + _dot(ectx_s[rows, :], vx_s[ctx, :])
        res = acc[:, :LANES] / acc[:, LANES:]
        for i in range(NAT_RB // blk):
            r = r0 // blk + i
            o_ref[0, r * GRID_W:(r + 1) * GRID_W, :] = unstack(res[i * blk:(i + 1) * blk], GRID_W)


def _nat_bias_table(rpb):
    n_dr, n_dc = 2 * WIN_ROWS - 1, 2 * WIN_COLS - 1
    rpb = rpb.reshape(DEPTH * NAT_HEADS, n_dr, n_dc)
    w = np.arange(GRID_W)
    cc = np.arange(GRID_W)
    start = np.clip(w - WIN_COLS // 2, 0, GRID_W - WIN_COLS)
    inside = (cc[None, :] >= start[:, None]) & (cc[None, :] < start[:, None] + WIN_COLS)
    d_col = cc[None, :] - w[:, None] + (WIN_COLS - 1)
    pick = ((np.arange(n_dc)[:, None, None] == d_col[None]) & inside[None]).astype(np.float32)
    toe = jnp.einsum('nrd,dwc->nwrc', rpb.astype(F32), pick, precision=lax.Precision.HIGHEST)
    toe = jnp.where(inside[None, :, None, :], toe * LOG2E, NEG)
    flat = toe.reshape(-1, GRID_W, n_dr * GRID_W)
    copy0 = jnp.pad(flat, ((0, 0), (0, 0), (0, NAT_BIAS_LANES - n_dr * GRID_W)))
    copy1 = jnp.pad(flat[:, :, GRID_W:], ((0, 0), (0, 0), (0, NAT_BIAS_LANES - (n_dr - 1) * GRID_W)))
    return jnp.stack([copy0, copy1], axis=1).reshape(DEPTH, NAT_HEADS, 2, GRID_W, NAT_BIAS_LANES)


def _nat_attention(z, bias, layer, ctx_queries):
    qb = Z_QN * TN // LANES
    kb = Z_KN * TN // LANES
    vb = Z_VN * TN // LANES
    return pl.pallas_call(
        functools.partial(_nat_kernel, ctx_queries=ctx_queries),
        grid=(B, NAT_HEADS // 2),
        in_specs=[
            pl.BlockSpec((1, T, LANES), lambda b, p: (b, 0, qb + p)),
            pl.BlockSpec((1, T, LANES), lambda b, p: (b, 0, kb + p)),
            pl.BlockSpec((1, T, LANES), lambda b, p: (b, 0, vb + p)),
            pl.BlockSpec((1, 2, 2, GRID_W, NAT_BIAS_LANES), lambda b, p: (layer, p, 0, 0, 0)),
        ],
        out_specs=pl.BlockSpec((1, T, LANES), lambda b, p: (b, 0, p)),
        out_shape=jax.ShapeDtypeStruct((B, T, BW), BF16),
        scratch_shapes=[
            pltpu.VMEM((LANES, T), BF16),
            pltpu.VMEM((T, 2 * LANES), BF16),
            pltpu.VMEM((2 * L, LANES), BF16),
            pltpu.VMEM((2 * C, LANES), BF16),
            pltpu.VMEM((2 * L, C), F32),
            pltpu.VMEM((2 * L, C), BF16),
            pltpu.VMEM((2 * L, 2 * LANES), F32),
            pltpu.VMEM((2 * L, N_LOC), F32),
            pltpu.VMEM((2 * L, N_LOC), BF16),
        ],
        compiler_params=_cparams(2),
        name="nat_attention",
    )(z, z, z, bias)


def _pool_kernel(u_ref, w_ref, sc_ref, o_ref):
    for lo, n in ((0, L), (L, C)):
        t = lax.broadcasted_iota(jnp.int32, (n, LANES), 0)
        for gi, win in enumerate(POOL_WINDOWS):
            half = win // 2
            cols = slice(gi * LANES, (gi + 1) * LANES)
            u = u_ref[0, lo:lo + n, cols].astype(F32)

            def up(a, k):
                return jnp.where(t < n - k, pltpu.roll(a, n - k, 0), 0.0)

            def down(a, k):
                return jnp.where(t >= k, pltpu.roll(a, k, 0), 0.0)

            fwd = u
            bwd = down(u, 1)
            span = 1
            while span < half:
                fwd = fwd + up(fwd, span)
                bwd = bwd + down(bwd, span)
                span *= 2
            cnt = (jnp.minimum(t + half, n) - jnp.maximum(t - half, 0)).astype(F32)
            p = (fwd + bwd) / cnt - u
            y = _dot(p.astype(BF16), w_ref[gi]) * sc_ref[:, cols]
            o_ref[0, lo:lo + n, cols] = y.astype(BF16)


def _pool(z, pool_w, pool_scale):
    return pl.pallas_call(
        _pool_kernel,
        grid=(B,),
        in_specs=[
            pl.BlockSpec((1, T, BW), lambda b: (b, 0, Z_PU)),
            pl.BlockSpec((len(POOL_WINDOWS), LANES, LANES), lambda b: (0, 0, 0)),
            pl.BlockSpec((1, BW), lambda b: (0, 0)),
        ],
        out_specs=pl.BlockSpec((1, T, BW), lambda b: (b, 0, 0)),
        out_shape=jax.ShapeDtypeStruct((B, T, BW), BF16),
        compiler_params=_cparams(1),
        name="pool",
    )(z, pool_w, pool_scale)


def _mod_row(mod_ref, idx, is_ctx):
    lat = mod_ref[0, 0, idx:idx + 1, :]
    if is_ctx is None:
        return lat
    return jnp.where(is_ctx, mod_ref[0, 1, idx:idx + 1, :], lat)


def _ctx_flag(i, tm, has_ctx, row0):
    if not has_ctx or row0 != tm - C:
        return None
    return i == T // tm - 1


def _merge_kernel(oa_ref, ob_ref, oc_ref, g_ref, x_ref, mod_ref, wb_ref, wo_ref, o_ref, *, tm, has_ctx):
    i = pl.program_id(1)
    merged = None
    for br, ref in enumerate((oa_ref, ob_ref, oc_ref)):
        gate = g_ref[0, :, br * D:(br + 1) * D].astype(F32)
        term = gate * _dot(ref[0], wb_ref[0, br])
        merged = term if merged is None else merged + term
    y = _dot(merged.astype(BF16), wo_ref[0])
    for r0 in range(0, tm, RC):
        rows = slice(r0, r0 + RC)
        gate1 = _mod_row(mod_ref, 2, _ctx_flag(i, tm, has_ctx, r0))
        o_ref[0, rows, :] = x_ref[0, rows, :] + gate1 * y[rows]


def _merge(oa, ob, oc, z, xs, modb, wb, wo, layer, last):
    tm = TM_LAST if last else TM
    n_rows = L if last else T
    row_spec = lambda w: pl.BlockSpec((1, tm, w), lambda b, i: (b, i, 0))
    return pl.pallas_call(
        functools.partial(_merge_kernel, tm=tm, has_ctx=not last),
        grid=(B, n_rows // tm),
        in_specs=[
            row_spec(BW), row_spec(BW), row_spec(BW),
            pl.BlockSpec((1, tm, 3 * D), lambda b, i: (b, i, Z_GATE)),
            row_spec(D),
            pl.BlockSpec((1, 2, 6, D), lambda b, i: (b, 0, 0, 0)),
            pl.BlockSpec((1, 3, BW, D), lambda b, i: (layer, 0, 0, 0)),
            pl.BlockSpec((1, D, D), lambda b, i: (layer, 0, 0)),
        ],
        out_specs=row_spec(D),
        out_shape=jax.ShapeDtypeStruct((B, n_rows, D), F32),
        compiler_params=_cparams(2),
        name="merge",
    )(oa, ob, oc, z, xs, modb, wb, wo)


def _mlp_kernel(x_ref, mod_ref, g_ref, w1_ref, w2_ref, gf_ref, o_ref, h_s, *, tm, has_ctx, final_norm):
    i = pl.program_id(1)
    g = g_ref[...]
    for r0 in range(0, tm, RC):
        rows = slice(r0, r0 + RC)
        is_ctx = _ctx_flag(i, tm, has_ctx, r0)
        x = x_ref[0, rows, :]
        ms = jnp.mean(x * x, axis=-1, keepdims=True)
        y = x * lax.rsqrt(ms + EPS) * g
        h_s[rows, :] = (y * (1.0 + _mod_row(mod_ref, 4, is_ctx)) + _mod_row(mod_ref, 3, is_ctx)).astype(BF16)
    h = h_s[...]
    acc = None
    for f in range(D_FF // FC):
        cols = slice(f * FC, (f + 1) * FC)
        a = jnp.maximum(_dot(h, w1_ref[0, :, cols]), 0.0)
        part = _dot((a * a).astype(BF16), w2_ref[0, cols, :])
        acc = part if acc is None else acc + part
    for r0 in range(0, tm, RC):
        rows = slice(r0, r0 + RC)
        gate2 = _mod_row(mod_ref, 5, _ctx_flag(i, tm, has_ctx, r0))
        out = x_ref[0, rows, :] + gate2 * acc[rows]
        if final_norm:
            ms = jnp.mean(out * out, axis=-1, keepdims=True)
            out = out * lax.rsqrt(ms + EPS) * gf_ref[...]
        o_ref[0, rows, :] = out


def _mlp(xs, modb, g, w1, w2, g_final, layer, last):
    tm = TM_LAST if last else TM
    n_rows = xs.shape[1]
    row_spec = pl.BlockSpec((1, tm, D), lambda b, i: (b, i, 0))
    const = lambda shape: pl.BlockSpec((1,) + shape, lambda b, i: (layer, 0, 0), pipeline_mode=pl.Buffered(1))
    return pl.pallas_call(
        functools.partial(_mlp_kernel, tm=tm, has_ctx=not last, final_norm=last),
        grid=(B, n_rows // tm),
        in_specs=[
            row_spec,
            pl.BlockSpec((1, 2, 6, D), lambda b, i: (b, 0, 0, 0)),
            pl.BlockSpec((1, D), lambda b, i: (0, 0)),
            const((D, D_FF)),
            const((D_FF, D)),
            pl.BlockSpec((1, D), lambda b, i: (0, 0)),
        ],
        out_specs=row_spec,
        out_shape=jax.ShapeDtypeStruct((B, n_rows, D), F32),
        scratch_shapes=[pltpu.VMEM((tm, D), BF16)],
        compiler_params=_cparams(2),
        name="mlp",
    )(xs, modb, g, w1, w2, g_final)


def kernel(x, c, ctx, c_ctx, w_ada, b_ada, g_mix, g_mlp, w_in, lambda_q1, lambda_k1, lambda_q2, lambda_k2,
           g_subln, pool_w, pool_scale, nat_rpb, w_branch, w_out, w_mlp1, w_mlp2, g_final):
    xs = jnp.concatenate([x, ctx], axis=1)
    pad = jnp.zeros((16 - B - 1, D), F32)
    cond = jnp.concatenate([c, c_ctx[None, :], pad], axis=0)
    mod = _adaln(cond, w_ada, b_ada)
    cos_t, sin_t = _rope_tables()
    bias = _nat_bias_table(nat_rpb)
    wb, wo = w_branch.astype(BF16), w_out.astype(BF16)
    w1, w2 = w_mlp1.astype(BF16), w_mlp2.astype(BF16)

    for l in range(DEPTH):
        last = l == DEPTH - 1
        lam_init = 0.8 - 0.6 * math.exp(-0.3 * l)
        lat = mod[l, :B].reshape(B, 1, 6, D)
        cx = jnp.broadcast_to(mod[l, B].reshape(1, 1, 6, D), (B, 1, 6, D))
        modb = jnp.concatenate([lat, cx], axis=1)

        z = _inproj(xs, modb, g_mix[l][None, :], w_in, l, cos_t, sin_t, not last)
        lparams = jnp.stack([lambda_q1[l], lambda_k1[l], lambda_q2[l], lambda_k2[l]]).astype(F32)
        gs = g_subln[l].astype(F32)[None, :]
        oa = _diff_attention(z, jnp.full((1,), lam_init, F32), lparams, gs, not last)
        ob = _pool(z, pool_w[l].astype(BF16), pool_scale[l][None, :])
        oc = _nat_attention(z, bias, l, not last)
        xs = _merge(oa, ob, oc, z, xs, modb, wb, wo, l, last)
        xs = _mlp(xs, modb, g_mlp[l][None, :], w1, w2, g_final[None, :], l, last)
    return xs
```

```python
import functools
import math

import numpy as np
import jax
import jax.numpy as jnp
from jax import lax
from jax.experimental import pallas as pl
from jax.experimental.pallas import tpu as pltpu

F32 = jnp.float32
BF16 = jnp.bfloat16

D = 1024
B = 8
L = 2048
C = 256
T = L + C
DEPTH = 4
GRID_W = 64
ROWS = L // GRID_W
DH = 64
DIFF_HEADS = 4
NAT_HEADS = 8
BW = 512
WIN_ROWS = 8
WIN_COLS = 16
N_LOC = WIN_ROWS * GRID_W
POOL_WINDOWS = (2, 4, 8, 16)
POOL_PAD = max(POOL_WINDOWS)
D_FF = 4 * D
IN_COLS = 7 * BW + 3 * D
ROPE_BASE = 10000.0
EPS = 1e-6
NEG = -1e30
LOG2E = math.log2(math.e)
Q_SCALE = DH ** -0.5 * LOG2E

LANES = 128
VMEM_LIMIT = 56 * 1024 * 1024

RC = 256
TN = 512
NJ = IN_COLS // TN
DIFF_RB = 1024
KC = 256
NAT_RB = 1024
NAT_PB = 256
NAT_BIAS_LANES = 1024
TM = 768
TM_LAST = 1024
FC = 1024

Z_GATE, Z_QA, Z_KA, Z_VA, Z_PU, Z_QN, Z_KN, Z_VN = 0, 6, 7, 8, 9, 10, 11, 12


def _cparams(n_axes):
    return pltpu.CompilerParams(dimension_semantics=("arbitrary",) * n_axes,
                                vmem_limit_bytes=VMEM_LIMIT)


def _dot(a, b):
    return jnp.dot(a, b, preferred_element_type=F32)


ADA_TN = 1536


def _adaln_kernel(c_ref, w_ref, b_ref, o_ref):
    cond = c_ref[...]
    s = (cond * jax.nn.sigmoid(cond)).astype(BF16)
    o_ref[0] = _dot(s, w_ref[0].astype(BF16)) + b_ref[0]


def _adaln(cond, w_ada, b_ada):
    n = cond.shape[0]
    return pl.pallas_call(
        _adaln_kernel,
        grid=(DEPTH, 6 * D // ADA_TN),
        in_specs=[
            pl.BlockSpec((n, D), lambda l, j: (0, 0)),
            pl.BlockSpec((1, D, ADA_TN), lambda l, j: (l, 0, j)),
            pl.BlockSpec((1, 1, ADA_TN), lambda l, j: (l, 0, j)),
        ],
        out_specs=pl.BlockSpec((1, n, ADA_TN), lambda l, j: (l, 0, j)),
        out_shape=jax.ShapeDtypeStruct((DEPTH, n, 6 * D), F32),
        compiler_params=_cparams(2),
        name="adaln",
    )(cond, w_ada, b_ada.reshape(DEPTH, 1, 6 * D))


def _inproj_kernel(x_ref, xc_ref, mod_ref, g_ref, w_ref, cos_ref, sin_ref, o_ref, h_s, w_s, *, ctx_all):
    j = pl.program_id(1)
    w_s[...] = w_ref[0].astype(BF16)

    @pl.when(j == 0)
    def _():
        for r0 in range(0, T, RC):
            seg = 1 if r0 >= L else 0
            xr = xc_ref[0] if seg else x_ref[0, r0:r0 + RC, :]
            ms = jnp.mean(xr * xr, axis=-1, keepdims=True)
            y = xr * lax.rsqrt(ms + EPS) * g_ref[...]
            h_s[r0:r0 + RC, :] = (y * (1.0 + mod_ref[0, seg, 1:2, :]) + mod_ref[0, seg, 0:1, :]).astype(BF16)

    def run(epilogue, with_ctx=True):
        for r0 in range(0, T if with_ctx else L, RC):
            rows = slice(r0, r0 + RC)
            acc = _dot(h_s[rows, :], w_s[...])
            epilogue(acc, rows)
        if not with_ctx:
            o_ref[0, L:T, :] = jnp.zeros((C, TN), BF16)

    def plain(acc, rows):
        o_ref[0, rows, :] = acc.astype(BF16)

    def scaled(acc, rows):
        o_ref[0, rows, :] = (acc * Q_SCALE).astype(BF16)

    def gate(acc, rows):
        o_ref[0, rows, :] = jax.nn.sigmoid(acc).astype(BF16)

    def rope(pre):
        def epilogue(acc, rows):
            cs = cos_ref[rows, :]
            sn = sin_ref[rows, :]
            upper = (lax.broadcasted_iota(jnp.int32, (RC, LANES), 1) & 16) != 0
            for cc in range(TN // LANES):
                u = acc[:, cc * LANES:(cc + 1) * LANES]
                if pre != 1.0:
                    u = u * pre
                partner = jnp.where(upper, pltpu.roll(u, 16, 1), pltpu.roll(u, LANES - 16, 1))
                o_ref[0, rows, cc * LANES:(cc + 1) * LANES] = (u * cs + partner * sn).astype(BF16)
        return epilogue

    pl.when(j == 0)(lambda: run(rope(Q_SCALE), ctx_all))
    pl.when(j == 1)(lambda: run(rope(1.0)))
    pl.when(j == 4)(lambda: run(scaled, ctx_all))
    pl.when(j >= 7)(lambda: run(gate, ctx_all))
    pl.when((j == 2) | (j == 5) | (j == 6))(lambda: run(plain))
    pl.when(j == 3)(lambda: run(plain, ctx_all))


def _inproj(x_lat, x_ctx, ctx_block, modb, g, w_in, layer, cos_t, sin_t, ctx_all):
    return pl.pallas_call(
        functools.partial(_inproj_kernel, ctx_all=ctx_all),
        grid=(B, NJ),
        in_specs=[
            pl.BlockSpec((1, L, D), lambda b, j: (b, 0, 0)),
            pl.BlockSpec((1, C, D), lambda b, j: (b, ctx_block, 0)),
            pl.BlockSpec((1, 2, 6, D), lambda b, j: (b, 0, 0, 0)),
            pl.BlockSpec((1, D), lambda b, j: (0, 0)),
            pl.BlockSpec((1, D, TN), lambda b, j: (layer, 0, j)),
            pl.BlockSpec((T, LANES), lambda b, j: (0, 0)),
            pl.BlockSpec((T, LANES), lambda b, j: (0, 0)),
        ],
        out_specs=pl.BlockSpec((1, T, TN), lambda b, j: (b, 0, (j + 6) % NJ)),
        out_shape=jax.ShapeDtypeStruct((B, T, IN_COLS), BF16),
        scratch_shapes=[pltpu.VMEM((T, D), BF16), pltpu.VMEM((D, TN), BF16)],
        compiler_params=_cparams(2),
        name="inproj",
    )(x_lat, x_ctx, modb, g, w_in, cos_t, sin_t)


def _rope_tables():
    nf = DH // 4
    pos = jnp.arange(L)
    row = (pos // GRID_W).astype(F32)
    col = (pos % GRID_W).astype(F32)
    inv = ROPE_BASE ** (-jnp.arange(nf, dtype=F32) / nf)

    def part(p):
        ang = p[:, None] * inv
        cs = jnp.cos(ang).astype(F32)
        sn = jnp.sin(ang).astype(F32)
        return jnp.concatenate([cs, cs], axis=1), jnp.concatenate([-sn, sn], axis=1)

    cr, sr = part(row)
    cc, sc = part(col)
    cos_h = jnp.concatenate([cr, cc], axis=1)
    sin_h = jnp.concatenate([sr, sc], axis=1)
    reps = LANES // DH
    cos_t = jnp.concatenate([jnp.tile(cos_h, (1, reps)), jnp.ones((C, LANES), F32)], axis=0)
    sin_t = jnp.concatenate([jnp.tile(sin_h, (1, reps)), jnp.zeros((C, LANES), F32)], axis=0)
    return cos_t, sin_t


def _split_heads(k_ref, lo_s, hi_s):
    kt = k_ref[0].astype(F32).T
    row = lax.broadcasted_iota(jnp.int32, (LANES, T), 0)
    lo_s[...] = jnp.where(row < DH, kt, 0.0).astype(BF16)
    hi_s[...] = jnp.where(row >= DH, kt, 0.0).astype(BF16)


def _extend_values(v_ref, vx_s):
    vx_s[:, :LANES] = v_ref[0]
    vx_s[:, LANES:] = jnp.ones((T, LANES), BF16)


def _diff_kernel(lam0_ref, q_ref, k_ref, v_ref, lp_ref, gs_ref, o_ref, k0_s, k1_s, vx_s, s0_s, s1_s, *, ctx_queries):
    _split_heads(k_ref, k0_s, k1_s)
    _extend_values(v_ref, vx_s)

    lam_init = lam0_ref[0]
    lp = lp_ref[...]
    lam = (jnp.exp(jnp.sum(lp[0:1] * lp[1:2], axis=-1, keepdims=True))
           - jnp.exp(jnp.sum(lp[2:3] * lp[3:4], axis=-1, keepdims=True)) + lam_init)
    maps = ((k0_s, s0_s), (k1_s, s1_s))

    def attend(r_lo, r_hi, k_lo, k_hi, kcs):
        rows = slice(r_lo, r_hi)
        srows = slice(0, r_hi - r_lo)
        q = q_ref[0, rows, :]
        chunks = [(c0, slice(c0 - k_lo, c0 - k_lo + kcs)) for c0 in range(k_lo, k_hi, kcs)]
        m = [None, None]
        for c0, cols in chunks:
            for mi, (ks, ss) in enumerate(maps):
                s = _dot(q, ks[:, c0:c0 + kcs])
                ss[srows, cols] = s
                mc = jnp.max(s, axis=-1, keepdims=True)
                m[mi] = mc if m[mi] is None else jnp.maximum(m[mi], mc)
        acc = [None, None]
        for c0, cols in chunks:
            for mi, (ks, ss) in enumerate(maps):
                e = jnp.exp2(ss[srows, cols] - m[mi]).astype(BF16)
                part = _dot(e, vx_s[c0:c0 + kcs, :])
                acc[mi] = part if acc[mi] is None else acc[mi] + part
        heads = [a[:, :LANES] / a[:, LANES:] for a in acc]
        o = heads[0] - lam * heads[1]
        ms = jnp.mean(o * o, axis=-1, keepdims=True)
        y = o * lax.rsqrt(ms + EPS) * gs_ref[...]
        o_ref[0, rows, :] = (y * (1.0 - lam_init)).astype(BF16)

    for r0 in range(0, L, DIFF_RB):
        attend(r0, r0 + DIFF_RB, 0, T, KC)
    if ctx_queries:
        attend(L, T, L, T, C)
    else:
        o_ref[0, L:T, :] = jnp.zeros((C, LANES), BF16)


def _diff_attention(z, lam0, lparams, g_subln, ctx_queries):
    qb = Z_QA * TN // LANES
    kb = Z_KA * TN // LANES
    vb = Z_VA * TN // LANES
    return pl.pallas_call(
        functools.partial(_diff_kernel, ctx_queries=ctx_queries),
        grid=(B, DIFF_HEADS),
        in_specs=[
            pl.BlockSpec(memory_space=pltpu.SMEM),
            pl.BlockSpec((1, T, LANES), lambda b, h: (b, 0, qb + h)),
            pl.BlockSpec((1, T, LANES), lambda b, h: (b, 0, kb + h)),
            pl.BlockSpec((1, T, LANES), lambda b, h: (b, 0, vb + h)),
            pl.BlockSpec((4, DH), lambda b, h: (0, 0)),
            pl.BlockSpec((1, LANES), lambda b, h: (0, 0)),
        ],
        out_specs=pl.BlockSpec((1, T, LANES), lambda b, h: (b, 0, h)),
        out_shape=jax.ShapeDtypeStruct((B, T, BW), BF16),
        scratch_shapes=[
            pltpu.VMEM((LANES, T), BF16),
            pltpu.VMEM((LANES, T), BF16),
            pltpu.VMEM((T, 2 * LANES), BF16),
            pltpu.VMEM((DIFF_RB, T), F32),
            pltpu.VMEM((DIFF_RB, T), F32),
        ],
        compiler_params=_cparams(2),
        name="diff_attention",
    )(lam0, z, z, z, lparams, g_subln)


def _nat_kernel(q_ref, k_ref, v_ref, bias_ref, o_ref, kt_s, vx_s, qq_s, qc_s, sctx_s, ectx_s, acc_s,
                sloc_s, eloc_s, *, ctx_queries):
    kt_s[...] = k_ref[0].astype(F32).T.astype(BF16)
    _extend_values(v_ref, vx_s)
    ctx = slice(L, T)
    blk = 2 * GRID_W

    def stack_heads(dst, n_blocks, src_lo, rows_per_block):
        for i in range(n_blocks):
            qf = q_ref[0, src_lo + i * rows_per_block:src_lo + (i + 1) * rows_per_block, :].astype(F32)
            low = lax.broadcasted_iota(jnp.int32, qf.shape, 1) < DH
            dst[2 * i * rows_per_block:(2 * i + 1) * rows_per_block, :] = jnp.where(low, qf, 0.0).astype(BF16)
            dst[(2 * i + 1) * rows_per_block:(2 * i + 2) * rows_per_block, :] = jnp.where(low, 0.0, qf).astype(BF16)

    def unstack(res, n):
        low = lax.broadcasted_iota(jnp.int32, (n, LANES), 1) < DH
        return jnp.where(low, res[:n], res[n:]).astype(BF16)

    stack_heads(qq_s, ROWS, 0, GRID_W)

    if ctx_queries:
        stack_heads(qc_s, 1, L, C)
        s = _dot(qc_s[...], kt_s[:, ctx])
        e = jnp.exp2(s - jnp.max(s, axis=-1, keepdims=True)).astype(BF16)
        acc = _dot(e, vx_s[ctx, :])
        o_ref[0, ctx, :] = unstack(acc[:, :LANES] / acc[:, LANES:], C)
    else:
        o_ref[0, ctx, :] = jnp.zeros((C, LANES), BF16)

    for r0 in range(0, 2 * L, NAT_RB):
        sctx_s[r0:r0 + NAT_RB, :] = _dot(qq_s[r0:r0 + NAT_RB, :], kt_s[:, ctx])

    def row_window(r):
        rs = min(max(r - WIN_ROWS // 2, 0), ROWS - WIN_ROWS)
        return slice(r * blk, (r + 1) * blk), slice(rs * GRID_W, rs * GRID_W + N_LOC)

    def local_scores(r):
        rs = min(max(r - WIN_ROWS // 2, 0), ROWS - WIN_ROWS)
        rows, keys = row_window(r)
        start = (WIN_ROWS - 1 - (r - rs)) * GRID_W
        copy, start = (0, start) if start % LANES == 0 else (1, start - GRID_W)
        bias = jnp.concatenate([bias_ref[0, hh, copy, :, start:start + N_LOC] for hh in range(2)], axis=0)
        return rows, keys, _dot(qq_s[rows, :], kt_s[:, keys]) + bias

    for r in range(ROWS):
        rows, _, s_loc = local_scores(r)
        sloc_s[rows, :] = s_loc
    for r0 in range(0, 2 * L, NAT_PB):
        rows = slice(r0, r0 + NAT_PB)
        s_loc = sloc_s[rows, :]
        s_ctx = sctx_s[rows, :]
        m = jnp.maximum(jnp.max(s_loc, axis=-1, keepdims=True), jnp.max(s_ctx, axis=-1, keepdims=True))
        ectx_s[rows, :] = jnp.exp2(s_ctx - m).astype(BF16)
        eloc_s[rows, :] = jnp.exp2(s_loc - m).astype(BF16)
    for r in range(ROWS):
        rows, keys = row_window(r)
        acc_s[rows, :] = _dot(eloc_s[rows, :], vx_s[keys, :])

    for r0 in range(0, 2 * L, NAT_RB):
        rows = slice(r0, r0 + NAT_RB)
        acc = acc_s[rows, :] + _dot(ectx_s[rows, :], vx_s[ctx, :])
        res = acc[:, :LANES] / acc[:, LANES:]
        for i in range(NAT_RB // blk):
            r = r0 // blk + i
            o_ref[0, r * GRID_W:(r + 1) * GRID_W, :] = unstack(res[i * blk:(i + 1) * blk], GRID_W)


def _nat_bias_table(rpb):
    n_dr, n_dc = 2 * WIN_ROWS - 1, 2 * WIN_COLS - 1
    rpb = rpb.reshape(DEPTH * NAT_HEADS, n_dr, n_dc)
    w = np.arange(GRID_W)
    cc = np.arange(GRID_W)
    start = np.clip(w - WIN_COLS // 2, 0, GRID_W - WIN_COLS)
    inside = (cc[None, :] >= start[:, None]) & (cc[None, :] < start[:, None] + WIN_COLS)
    d_col = cc[None, :] - w[:, None] + (WIN_COLS - 1)
    pick = ((np.arange(n_dc)[:, None, None] == d_col[None]) & inside[None]).astype(np.float32)
    toe = jnp.einsum('nrd,dwc->nwrc', rpb.astype(F32), pick, precision=lax.Precision.HIGHEST)
    toe = jnp.where(inside[None, :, None, :], toe * LOG2E, NEG)
    flat = toe.reshape(-1, GRID_W, n_dr * GRID_W)
    copy0 = jnp.pad(flat, ((0, 0), (0, 0), (0, NAT_BIAS_LANES - n_dr * GRID_W)))
    copy1 = jnp.pad(flat[:, :, GRID_W:], ((0, 0), (0, 0), (0, NAT_BIAS_LANES - (n_dr - 1) * GRID_W)))
    return jnp.stack([copy0, copy1], axis=1).reshape(DEPTH, NAT_HEADS, 2, GRID_W, NAT_BIAS_LANES)


def _nat_attention(z, bias, layer, ctx_queries):
    qb = Z_QN * TN // LANES
    kb = Z_KN * TN // LANES
    vb = Z_VN * TN // LANES
    return pl.pallas_call(
        functools.partial(_nat_kernel, ctx_queries=ctx_queries),
        grid=(B, NAT_HEADS // 2),
        in_specs=[
            pl.BlockSpec((1, T, LANES), lambda b, p: (b, 0, qb + p)),
            pl.BlockSpec((1, T, LANES), lambda b, p: (b, 0, kb + p)),
            pl.BlockSpec((1, T, LANES), lambda b, p: (b, 0, vb + p)),
            pl.BlockSpec((1, 2, 2, GRID_W, NAT_BIAS_LANES), lambda b, p: (layer, p, 0, 0, 0)),
        ],
        out_specs=pl.BlockSpec((1, T, LANES), lambda b, p: (b, 0, p)),
        out_shape=jax.ShapeDtypeStruct((B, T, BW), BF16),
        scratch_shapes=[
            pltpu.VMEM((LANES, T), BF16),
            pltpu.VMEM((T, 2 * LANES), BF16),
            pltpu.VMEM((2 * L, LANES), BF16),
            pltpu.VMEM((2 * C, LANES), BF16),
            pltpu.VMEM((2 * L, C), F32),
            pltpu.VMEM((2 * L, C), BF16),
            pltpu.VMEM((2 * L, 2 * LANES), F32),
            pltpu.VMEM((2 * L, N_LOC), F32),
            pltpu.VMEM((2 * L, N_LOC), BF16),
        ],
        compiler_params=_cparams(2),
        name="nat_attention",
    )(z, z, z, bias)


def _pool_kernel(u_ref, w_ref, sc_ref, o_ref):
    for lo, n in ((0, L), (L, C)):
        t = lax.broadcasted_iota(jnp.int32, (n, LANES), 0)
        for gi, win in enumerate(POOL_WINDOWS):
            half = win // 2
            cols = slice(gi * LANES, (gi + 1) * LANES)
            u = u_ref[0, lo:lo + n, cols].astype(F32)
            s = jnp.concatenate([u, jnp.zeros((POOL_PAD, LANES), F32)], axis=0)
            span = 1
            while span < win:
                s = s + pltpu.roll(s, span, 0)
                span *= 2
            if half > 1:
                s = pltpu.roll(s, n + POOL_PAD - (half - 1), 0)
            cnt = (jnp.minimum(t + half, n) - jnp.maximum(t - half, 0)).astype(F32)
            p = s[:n] / cnt - u
            y = _dot(p.astype(BF16), w_ref[gi]) * sc_ref[:, cols]
            o_ref[0, lo:lo + n, cols] = y.astype(BF16)


def _pool(z, pool_w, pool_scale):
    return pl.pallas_call(
        _pool_kernel,
        grid=(B,),
        in_specs=[
            pl.BlockSpec((1, T, BW), lambda b: (b, 0, Z_PU)),
            pl.BlockSpec((len(POOL_WINDOWS), LANES, LANES), lambda b: (0, 0, 0)),
            pl.BlockSpec((1, BW), lambda b: (0, 0)),
        ],
        out_specs=pl.BlockSpec((1, T, BW), lambda b: (b, 0, 0)),
        out_shape=jax.ShapeDtypeStruct((B, T, BW), BF16),
        compiler_params=_cparams(1),
        name="pool",
    )(z, pool_w, pool_scale)


def _mod_row(mod_ref, idx, is_ctx):
    lat = mod_ref[0, 0, idx:idx + 1, :]
    if is_ctx is None:
        return lat
    return jnp.where(is_ctx, mod_ref[0, 1, idx:idx + 1, :], lat)


def _ctx_flag(i, tm, has_ctx, row0):
    if not has_ctx or row0 != tm - C:
        return None
    return i == T // tm - 1


def _merge_kernel(oa_ref, ob_ref, oc_ref, g_ref, x_ref, xc_ref, mod_ref, wb_ref, wo_ref, o_ref, *, tm, has_ctx):
    i = pl.program_id(1)
    merged = None
    for br, ref in enumerate((oa_ref, ob_ref, oc_ref)):
        gate = g_ref[0, :, br * D:(br + 1) * D].astype(F32)
        term = gate * _dot(ref[0], wb_ref[0, br].astype(BF16))
        merged = term if merged is None else merged + term
    y = _dot(merged.astype(BF16), wo_ref[0].astype(BF16))
    for r0 in range(0, tm, RC):
        rows = slice(r0, r0 + RC)
        yr = y[rows]
        if has_ctx and r0 == tm - C:
            is_ctx = i == T // tm - 1

            @pl.when(is_ctx)
            def _():
                o_ref[0, rows, :] = xc_ref[0] + mod_ref[0, 1, 2:3, :] * yr

            @pl.when(jnp.logical_not(is_ctx))
            def _():
                o_ref[0, rows, :] = x_ref[0, rows, :] + mod_ref[0, 0, 2:3, :] * yr
        else:
            o_ref[0, rows, :] = x_ref[0, rows, :] + mod_ref[0, 0, 2:3, :] * yr


def _merge(oa, ob, oc, z, x_lat, x_ctx, ctx_block, modb, wb, wo, layer, last):
    tm = TM_LAST if last else TM
    n_rows = L if last else T
    row_spec = lambda w: pl.BlockSpec((1, tm, w), lambda b, i: (b, i, 0))
    return pl.pallas_call(
        functools.partial(_merge_kernel, tm=tm, has_ctx=not last),
        grid=(B, n_rows // tm),
        in_specs=[
            row_spec(BW), row_spec(BW), row_spec(BW),
            pl.BlockSpec((1, tm, 3 * D), lambda b, i: (b, i, Z_GATE)),
            row_spec(D),
            pl.BlockSpec((1, C, D), lambda b, i: (b, ctx_block, 0)),
            pl.BlockSpec((1, 2, 6, D), lambda b, i: (b, 0, 0, 0)),
            pl.BlockSpec((1, 3, BW, D), lambda b, i: (layer, 0, 0, 0), pipeline_mode=pl.Buffered(1)),
            pl.BlockSpec((1, D, D), lambda b, i: (layer, 0, 0), pipeline_mode=pl.Buffered(1)),
        ],
        out_specs=row_spec(D),
        out_shape=jax.ShapeDtypeStruct((B, n_rows, D), F32),
        compiler_params=_cparams(2),
        name="merge",
    )(oa, ob, oc, z, x_lat, x_ctx, modb, wb, wo)


def _mlp_kernel(x_ref, mod_ref, g_ref, w1_ref, w2_ref, gf_ref, o_ref, h_s, *, tm, has_ctx, final_norm):
    i = pl.program_id(1)
    g = g_ref[...]
    for r0 in range(0, tm, RC):
        rows = slice(r0, r0 + RC)
        is_ctx = _ctx_flag(i, tm, has_ctx, r0)
        x = x_ref[0, rows, :]
        ms = jnp.mean(x * x, axis=-1, keepdims=True)
        y = x * lax.rsqrt(ms + EPS) * g
        h_s[rows, :] = (y * (1.0 + _mod_row(mod_ref, 4, is_ctx)) + _mod_row(mod_ref, 3, is_ctx)).astype(BF16)
    h = h_s[...]
    acc = None
    for f in range(D_FF // FC):
        cols = slice(f * FC, (f + 1) * FC)
        a = jnp.maximum(_dot(h, w1_ref[0, :, cols]), 0.0)
        part = _dot((a * a).astype(BF16), w2_ref[0, cols, :])
        acc = part if acc is None else acc + part
    for r0 in range(0, tm, RC):
        rows = slice(r0, r0 + RC)
        gate2 = _mod_row(mod_ref, 5, _ctx_flag(i, tm, has_ctx, r0))
        out = x_ref[0, rows, :] + gate2 * acc[rows]
        if final_norm:
            ms = jnp.mean(out * out, axis=-1, keepdims=True)
            out = out * lax.rsqrt(ms + EPS) * gf_ref[...]
        o_ref[0, rows, :] = out


def _mlp(xs, modb, g, w1, w2, g_final, layer, last):
    tm = TM_LAST if last else TM
    n_rows = xs.shape[1]
    row_spec = pl.BlockSpec((1, tm, D), lambda b, i: (b, i, 0))
    const = lambda shape: pl.BlockSpec((1,) + shape, lambda b, i: (layer, 0, 0), pipeline_mode=pl.Buffered(1))
    return pl.pallas_call(
        functools.partial(_mlp_kernel, tm=tm, has_ctx=not last, final_norm=last),
        grid=(B, n_rows // tm),
        in_specs=[
            row_spec,
            pl.BlockSpec((1, 2, 6, D), lambda b, i: (b, 0, 0, 0)),
            pl.BlockSpec((1, D), lambda b, i: (0, 0)),
            const((D, D_FF)),
            const((D_FF, D)),
            pl.BlockSpec((1, D), lambda b, i: (0, 0)),
        ],
        out_specs=row_spec,
        out_shape=jax.ShapeDtypeStruct((B, n_rows, D), F32),
        scratch_shapes=[pltpu.VMEM((tm, D), BF16)],
        compiler_params=_cparams(2),
        name="mlp",
    )(xs, modb, g, w1, w2, g_final)


def kernel(x, c, ctx, c_ctx, w_ada, b_ada, g_mix, g_mlp, w_in, lambda_q1, lambda_k1, lambda_q2, lambda_k2,
           g_subln, pool_w, pool_scale, nat_rpb, w_branch, w_out, w_mlp1, w_mlp2, g_final):
    x_lat, x_ctx, ctx_block = x, ctx, 0
    pad = jnp.zeros((16 - B - 1, D), F32)
    cond = jnp.concatenate([c, c_ctx[None, :], pad], axis=0)
    mod = _adaln(cond, w_ada, b_ada)
    cos_t, sin_t = _rope_tables()
    bias = _nat_bias_table(nat_rpb)
    w1, w2 = w_mlp1.astype(BF16), w_mlp2.astype(BF16)

    for l in range(DEPTH):
        last = l == DEPTH - 1
        lam_init = 0.8 - 0.6 * math.exp(-0.3 * l)
        lat = mod[l, :B].reshape(B, 1, 6, D)
        cx = jnp.broadcast_to(mod[l, B].reshape(1, 1, 6, D), (B, 1, 6, D))
        modb = jnp.concatenate([lat, cx], axis=1)

        z = _inproj(x_lat, x_ctx, ctx_block, modb, g_mix[l][None, :], w_in, l, cos_t, sin_t, not last)
        lparams = jnp.stack([lambda_q1[l], lambda_k1[l], lambda_q2[l], lambda_k2[l]]).astype(F32)
        gs = g_subln[l].astype(F32)[None, :]
        oa = _diff_attention(z, jnp.full((1,), lam_init, F32), lparams, gs, not last)
        ob = _pool(z, pool_w[l].astype(BF16), pool_scale[l][None, :])
        oc = _nat_attention(z, bias, l, not last)
        xs = _merge(oa, ob, oc, z, x_lat, x_ctx, ctx_block, modb, w_branch, w_out, l, last)
        xs = _mlp(xs, modb, g_mlp[l][None, :], w1, w2, g_final[None, :], l, last)
        x_lat, x_ctx, ctx_block = xs, xs, L // C
    return xs
```
